```python
import math
import jax, jax.numpy as jnp
from jax import lax
import numpy as np

D_MODEL = 4096
BATCH = 4
SEQ = 2048
DEPTH = 2
DEC_BATCH = 8
DEC_SEQ = 4
PAST_LEN = 16384
PAGE_SIZE = 128

D_MIX = D_MODEL
HEAD_DIM = 128
W_A = D_MIX // 2
H_A = W_A // HEAD_DIM
W_B = D_MIX // 4
G_B = W_B // HEAD_DIM
CHUNK = 128
W_C = D_MIX - W_A - W_B
NB_C = W_C // HEAD_DIM
BS_C = W_C // NB_C
CONV_W = 4
LRU_C = 8.0
N_MEM = 256
H_X = 4
HD_X = 128
W_X = H_X * HD_X
QBLK = 128
EPS = 1e-6
N_IN = 4 * W_A + 3 * W_B + 2 * W_C
SPLIT_POINTS = (W_A, 2 * W_A, 3 * W_A, 4 * W_A, 4 * W_A + W_B, 4 * W_A + 2 * W_B,
                4 * W_A + 3 * W_B, 4 * W_A + 3 * W_B + W_C)
SB_BIAS_LO = -10.0
SB_BIAS_HI = -5.0

kernel_name = "hybrid_sb_sgu_rglru_decoder_step"


def rmsnorm(x, g):
    x32 = x.astype(jnp.float32)
    y = x32 * lax.rsqrt(jnp.mean(x32 * x32, axis=-1, keepdims=True) + EPS)
    return (y * g.astype(jnp.float32)).astype(x.dtype)


def stick_breaking(q, k, v, sb_bias, q_pos0):
    Tq, Tk = q.shape[1], k.shape[1]
    scale = 1.0 / math.sqrt(q.shape[-1])
    bias = sb_bias.astype(jnp.float32)[None, :, None, None]
    outs = []
    for start in range(0, Tq, QBLK):
        qb = q[:, start:start + QBLK]
        nq = qb.shape[1]
        kend = min(Tk, q_pos0 + start + nq)
        kb, vb = k[:, :kend], v[:, :kend]
        z = jnp.einsum('bqhd,bkhd->bhqk', qb, kb).astype(jnp.float32) * scale + bias
        qpos = q_pos0 + start + jnp.arange(nq)
        kpos = jnp.arange(kend)
        mask = kpos[None, :] < qpos[:, None]
        log_beta = jax.nn.log_sigmoid(z)
        log_1m = jnp.where(mask, jax.nn.log_sigmoid(-z), 0.0)
        suffix = lax.cumsum(log_1m, axis=3, reverse=True) - log_1m
        a = jnp.where(mask, jnp.exp(log_beta + suffix), 0.0)
        o = jnp.einsum('bhqk,bkhd->bqhd', a, vb.astype(jnp.float32))
        outs.append(o.astype(q.dtype))
    return jnp.concatenate(outs, axis=1)


def chunk_spatial_mix(v, w_s, b_s):
    bn, t = v.shape[0], v.shape[1]
    n_chunks = -(-t // CHUNK)
    pad = n_chunks * CHUNK - t
    vp = jnp.pad(v, ((0, 0), (0, pad), (0, 0))).reshape(bn, n_chunks, CHUNK, G_B, W_B // G_B)
    mask = jnp.tril(jnp.ones((CHUNK, CHUNK), dtype=bool))
    w = jnp.where(mask[None], w_s, 0.0)
    mixed = jnp.einsum('gts,bcsgd->bctgd', w, vp) + jnp.transpose(b_s)[None, None, :, :, None]
    return mixed.reshape(bn, n_chunks * CHUNK, W_B)[:, :t]


def causal_conv(x, prev, w, b):
    t = x.shape[1]
    xp = jnp.concatenate([prev.astype(x.dtype), x], axis=1)
    y = xp[:, CONV_W - 1:CONV_W - 1 + t] * w[CONV_W - 1] + b
    for j in range(CONV_W - 1):
        y = y + xp[:, j:j + t] * w[j]
    return y, xp[:, -(CONV_W - 1):]


def _lin_combine(c1, c2):
    a1, b1 = c1
    a2, b2 = c2
    return a1 * a2, a2 * b1 + b2


def rg_lru(x, h0, w_r, b_r, w_i, b_i, lam):
    bn, t = x.shape[0], x.shape[1]
    x32 = x.astype(jnp.float32)
    xb = x32.reshape(bn, t, NB_C, BS_C)
    r = jax.nn.sigmoid(jnp.einsum('btnd,nde->btne', xb, w_r.astype(jnp.float32)).reshape(bn, t, W_C) + b_r)
    i = jax.nn.sigmoid(jnp.einsum('btnd,nde->btne', xb, w_i.astype(jnp.float32)).reshape(bn, t, W_C) + b_i)
    log_a = -LRU_C * r * jax.nn.softplus(-lam.astype(jnp.float32))
    a = jnp.exp(log_a)
    u = jnp.sqrt(-jnp.expm1(2.0 * log_a)) * (i * x32)
    u = u.at[:, 0].add(a[:, 0] * h0.astype(jnp.float32))
    _, h = lax.associative_scan(_lin_combine, (a, u), axis=1)
    return h.astype(x.dtype), h[:, -1].astype(x.dtype)


def mixer_sublayer(x, q_pos0, past_k, past_v, h0, conv_prev, g_mix, w_in, g_branch, w_out,
                   g_sgu, w_s, b_s, conv_w, conv_b, w_r, b_r, w_i, b_i, lam, sb_bias):
    bn, t = x.shape[0], x.shape[1]
    hn = rmsnorm(x, g_mix)
    proj = hn @ w_in
    qa, ka, va, za, ub, vb, zb, xc, zc = jnp.split(proj, SPLIT_POINTS, axis=-1)
    qa = qa.reshape(bn, t, H_A, HEAD_DIM)
    ka = ka.reshape(bn, t, H_A, HEAD_DIM)
    va = va.reshape(bn, t, H_A, HEAD_DIM)
    if past_k is None:
        keys, vals = ka, va
    else:
        keys = jnp.concatenate([past_k.astype(ka.dtype), ka], axis=1)
        vals = jnp.concatenate([past_v.astype(va.dtype), va], axis=1)
    ya = stick_breaking(qa, keys, vals, sb_bias, q_pos0).reshape(bn, t, W_A)
    vb_n = rmsnorm(vb, g_sgu)
    yb = ub * chunk_spatial_mix(vb_n, w_s, b_s)
    xc_c, conv_new = causal_conv(xc, conv_prev, conv_w, conv_b)
    yc, h_last = rg_lru(xc_c, h0, w_r, b_r, w_i, b_i, lam)
    ya = rmsnorm(ya, g_branch[:W_A]) * jax.nn.silu(za)
    yb = rmsnorm(yb, g_branch[W_A:W_A + W_B]) * jax.nn.silu(zb)
    yc = rmsnorm(yc, g_branch[W_A + W_B:]) * jax.nn.silu(zc)
    y = jnp.concatenate([ya, yb, yc], axis=-1) @ w_out
    return x + y, ka, va, vb_n, h_last, conv_new


def cross_attn(x, mem_k, mem_v, g_cross, w_cq, w_co):
    bn, t = x.shape[0], x.shape[1]
    hn = rmsnorm(x, g_cross)
    q = (hn @ w_cq).reshape(bn, t, H_X, HD_X)
    s = jnp.einsum('bqhd,bmhd->bhqm', q, mem_k.astype(q.dtype)).astype(jnp.float32) / math.sqrt(HD_X)
    p = jax.nn.softmax(s, axis=-1)
    o = jnp.einsum('bhqm,bmhd->bqhd', p, mem_v.astype(jnp.float32)).astype(x.dtype).reshape(bn, t, W_X)
    return x + o @ w_co


def setup_inputs(seed: int = 0) -> dict:
    key = jax.random.key(seed)
    ks = jax.random.split(key, 40)
    n_pages = PAST_LEN // PAGE_SIZE
    n_used = DEC_BATCH * n_pages
    n_pool = n_used + n_used // 4
    f32 = jnp.float32

    def nrm(k, shape, scale=1.0):
        return jax.random.normal(k, shape, f32) * scale

    page_table = jax.random.permutation(ks[0], n_pool)[:n_used].reshape(DEC_BATCH, n_pages).astype(jnp.int32)
    a8 = jax.random.uniform(ks[1], (DEPTH, W_C), f32, 0.9, 0.999)
    a_base = a8 ** (1.0 / LRU_C)
    lam = jnp.log(a_base) - jnp.log1p(-a_base)
    return {
        "x_prompt": nrm(ks[2], (BATCH, SEQ, D_MODEL)),
        "x_sample": nrm(ks[3], (DEC_BATCH, DEC_SEQ, D_MODEL)),
        "mem_prompt": nrm(ks[4], (BATCH, N_MEM, D_MODEL)),
        "cache_k": nrm(ks[5], (DEPTH, n_pool, PAGE_SIZE, H_A, HEAD_DIM)),
        "cache_v": nrm(ks[6], (DEPTH, n_pool, PAGE_SIZE, H_A, HEAD_DIM)),
        "page_table": page_table,
        "cache_mem_k": nrm(ks[7], (DEPTH, DEC_BATCH, N_MEM, H_X, HD_X)),
        "cache_mem_v": nrm(ks[8], (DEPTH, DEC_BATCH, N_MEM, H_X, HD_X)),
        "state_h": nrm(ks[9], (DEPTH, DEC_BATCH, W_C), 0.5),
        "state_conv": nrm(ks[10], (DEPTH, DEC_BATCH, CONV_W - 1, W_C)),
        "g_mix": 1.0 + nrm(ks[11], (DEPTH, D_MODEL), 0.02),
        "w_in": nrm(ks[12], (DEPTH, D_MODEL, N_IN), D_MODEL ** -0.5),
        "g_branch": 1.0 + nrm(ks[13], (DEPTH, D_MIX), 0.02),
        "w_out": nrm(ks[14], (DEPTH, D_MIX, D_MODEL), D_MIX ** -0.5),
        "g_sgu": 1.0 + nrm(ks[15], (DEPTH, W_B), 0.02),
        "w_s": nrm(ks[16], (DEPTH, G_B, CHUNK, CHUNK), CHUNK ** -0.5),
        "b_s": 1.0 + nrm(ks[17], (DEPTH, G_B, CHUNK), 0.02),
        "conv_w": nrm(ks[18], (DEPTH, CONV_W, W_C), CONV_W ** -0.5),
        "conv_b": nrm(ks[19], (DEPTH, W_C), 0.01),
        "w_r": nrm(ks[20], (DEPTH, NB_C, BS_C, BS_C), BS_C ** -0.5),
        "b_r": nrm(ks[21], (DEPTH, W_C), 0.01),
        "w_i": nrm(ks[22], (DEPTH, NB_C, BS_C, BS_C), BS_C ** -0.5),
        "b_i": nrm(ks[23], (DEPTH, W_C), 0.01),
        "lru_lambda": lam,
        "sb_bias": jax.random.uniform(ks[30], (DEPTH, H_A), f32, SB_BIAS_LO, SB_BIAS_HI),
        "g_cross": 1.0 + nrm(ks[24], (DEPTH, D_MODEL), 0.02),
        "w_cq": nrm(ks[25], (DEPTH, D_MODEL, W_X), D_MODEL ** -0.5),
        "w_ck": nrm(ks[26], (DEPTH, D_MODEL, W_X), D_MODEL ** -0.5),
        "w_cv": nrm(ks[27], (DEPTH, D_MODEL, W_X), D_MODEL ** -0.5),
        "w_co": nrm(ks[28], (DEPTH, W_X, D_MODEL), W_X ** -0.5),
        "g_final": 1.0 + nrm(ks[29], (D_MODEL,), 0.02),
    }


def reference(x_prompt, x_sample, mem_prompt, cache_k, cache_v, page_table, cache_mem_k, cache_mem_v,
              state_h, state_conv, g_mix, w_in, g_branch, w_out, g_sgu, w_s, b_s, conv_w, conv_b,
              w_r, b_r, w_i, b_i, lru_lambda, sb_bias, g_cross, w_cq, w_ck, w_cv, w_co, g_final):
    bp = x_prompt.shape[0]
    db = x_sample.shape[0]
    past_len = page_table.shape[1] * PAGE_SIZE
    xp, xs = x_prompt, x_sample
    kp_l, vp_l, ks_l, vs_l, mk_l, mv_l = [], [], [], [], [], []
    hp_l, hs_l, cp_l, cs_l, sgu_l = [], [], [], [], []
    for l in range(DEPTH):
        lw = (g_mix[l], w_in[l], g_branch[l], w_out[l], g_sgu[l], w_s[l], b_s[l], conv_w[l], conv_b[l],
              w_r[l], b_r[l], w_i[l], b_i[l], lru_lambda[l], sb_bias[l])
        h0p = jnp.zeros((bp, W_C), xp.dtype)
        c0p = jnp.zeros((bp, CONV_W - 1, W_C), xp.dtype)
        xp, kp, vp, _, hp, cp = mixer_sublayer(xp, 0, None, None, h0p, c0p, *lw)
        mk = (mem_prompt @ w_ck[l]).reshape(bp, N_MEM, H_X, HD_X)
        mv = (mem_prompt @ w_cv[l]).reshape(bp, N_MEM, H_X, HD_X)
        xp = cross_attn(xp, mk, mv, g_cross[l], w_cq[l], w_co[l])
        past_k = cache_k[l][page_table].reshape(db, past_len, H_A, HEAD_DIM)
        past_v = cache_v[l][page_table].reshape(db, past_len, H_A, HEAD_DIM)
        xs, ks, vs, sgu_s, hs, cs = mixer_sublayer(xs, past_len, past_k, past_v, state_h[l], state_conv[l], *lw)
        xs = cross_attn(xs, cache_mem_k[l], cache_mem_v[l], g_cross[l], w_cq[l], w_co[l])
        kp_l.append(kp); vp_l.append(vp); ks_l.append(ks); vs_l.append(vs)
        mk_l.append(mk); mv_l.append(mv)
        hp_l.append(hp); hs_l.append(hs); cp_l.append(cp); cs_l.append(cs); sgu_l.append(sgu_s)
    y_prompt = rmsnorm(xp, g_final)
    y_sample = rmsnorm(xs, g_final)
    new_k_prompt = jnp.stack(kp_l)
    new_v_prompt = jnp.stack(vp_l)
    new_k_sample = jnp.stack(ks_l)
    new_v_sample = jnp.stack(vs_l)
    new_mem_k_prompt = jnp.stack(mk_l)
    new_mem_v_prompt = jnp.stack(mv_l)
    new_h_prompt = jnp.stack(hp_l)
    new_h_sample = jnp.stack(hs_l)
    new_conv_prompt = jnp.stack(cp_l)
    new_conv_sample = jnp.stack(cs_l)
    new_sgu_v_sample = jnp.stack(sgu_l)
    return (y_prompt, y_sample, new_k_prompt, new_v_prompt, new_k_sample, new_v_sample,
            new_mem_k_prompt, new_mem_v_prompt, new_h_prompt, new_h_sample,
            new_conv_prompt, new_conv_sample, new_sgu_v_sample)
```

```python
import functools
import math

import jax
import jax.numpy as jnp
from jax import lax
from jax.experimental import pallas as pl
from jax.experimental.pallas import tpu as pltpu

F32 = jnp.float32
BF16 = jnp.bfloat16

EPS = 1e-6
LRU_C = 8.0
SUBLANES = 8
LANES = 128
VMEM_LIMIT_BYTES = 56 * 1024 * 1024

ROWS_NORM = 256
ROWS_MM = 1024
COLS_MM = 512
ATTN_BLK = 256
LRU_ROWS = 256
CROSS_ROWS = 256
PAGES_PER_STEP = 4


def _cparams(*sem):
    return pltpu.CompilerParams(dimension_semantics=sem, vmem_limit_bytes=VMEM_LIMIT_BYTES)


def _softplus(x):
    return jnp.maximum(x, 0.0) + jnp.log1p(jnp.exp(-jnp.abs(x)))


def _rms(x, g):
    return x * lax.rsqrt(jnp.mean(x * x, axis=-1, keepdims=True) + EPS) * g


def _silu(x):
    return x * jax.nn.sigmoid(x)


def _split_bf16(x):
    hi = x.astype(BF16)
    lo = (x - hi.astype(F32)).astype(BF16)
    return hi, lo


def _rmsnorm_kernel(x_ref, g_ref, o_ref):
    o_ref[...] = _rms(x_ref[...], g_ref[...]).astype(o_ref.dtype)


def _rmsnorm(x, g, out_dtype):
    m, d = x.shape
    tm = min(m, ROWS_NORM)
    return pl.pallas_call(
        _rmsnorm_kernel,
        out_shape=jax.ShapeDtypeStruct((m, d), out_dtype),
        grid=(m // tm,),
        in_specs=[pl.BlockSpec((tm, d), lambda i: (i, 0)),
                  pl.BlockSpec((1, d), lambda i: (0, 0))],
        out_specs=pl.BlockSpec((tm, d), lambda i: (i, 0)),
        compiler_params=_cparams("parallel"),
        name="rmsnorm",
    )(x, g.reshape(1, d))


def _matmul_kernel(*refs, n_in, has_res):
    o_ref = refs[-1]
    acc = None
    for a_ref, w_ref in zip(refs[:n_in], refs[n_in:2 * n_in]):
        d = jnp.dot(a_ref[...].astype(BF16), w_ref[...].astype(BF16), preferred_element_type=F32)
        acc = d if acc is None else acc + d
    if has_res:
        acc = acc + refs[2 * n_in][...]
    o_ref[...] = acc


def _matmul(a_list, w_list, n, res=None, rows=ROWS_MM, name="matmul"):
    m = a_list[0].shape[0]
    tm = min(m, rows)
    tn = min(n, COLS_MM)
    in_specs, args = [], []
    for a in a_list:
        in_specs.append(pl.BlockSpec((tm, a.shape[1]), lambda i, j: (i, 0)))
        args.append(a)
    for a, (w, layer, rb) in zip(a_list, w_list):
        in_specs.append(pl.BlockSpec((None, a.shape[1], tn),
                                     functools.partial(lambda i, j, layer, rb: (layer, rb, j), layer=layer, rb=rb)))
        args.append(w)
    if res is not None:
        in_specs.append(pl.BlockSpec((tm, tn), lambda i, j: (i, j)))
        args.append(res)
    return pl.pallas_call(
        functools.partial(_matmul_kernel, n_in=len(a_list), has_res=res is not None),
        out_shape=jax.ShapeDtypeStruct((m, n), F32),
        grid=(m // tm, n // tn),
        in_specs=in_specs,
        out_specs=pl.BlockSpec((tm, tn), lambda i, j: (i, j)),
        compiler_params=_cparams("parallel", "arbitrary"),
        name=name,
    )(*args)


def _sb_block(z, tri, carry, mask):
    sp = _softplus(z)
    if mask is not None:
        sp = jnp.where(mask, sp, 0.0)
    hi, lo = _split_bf16(sp)
    tail = (jnp.dot(hi, tri, preferred_element_type=F32)
            + jnp.dot(lo, tri, preferred_element_type=F32))
    a = jnp.exp(z - tail - carry)
    if mask is not None:
        a = jnp.where(mask, a, 0.0)
    return a, tail[:, 0:1]


def _sb_prompt_kernel(bias_ref, q_ref, k_ref, v_ref, tri_ref, o_ref, kb_ref, vb_ref, *, blk, scale):
    h = pl.program_id(1)
    i = pl.program_id(2)

    @pl.when(i == 0)
    def _():
        kb_ref[...] = k_ref[...].astype(BF16)
        vb_ref[...] = v_ref[...].astype(BF16)

    q = q_ref[...].astype(BF16)
    bias = bias_ref[h]
    tri = tri_ref[...]

    def logits(j):
        start = pl.multiple_of(j * blk, blk)
        kblk = kb_ref[pl.ds(start, blk), :]
        s = lax.dot_general(q, kblk, (((1,), (1,)), ((), ())), preferred_element_type=F32)
        return s * scale + bias, vb_ref[pl.ds(start, blk), :]

    row = lax.broadcasted_iota(jnp.int32, (blk, blk), 0)
    col = lax.broadcasted_iota(jnp.int32, (blk, blk), 1)
    z, vblk = logits(i)
    a, carry = _sb_block(z, tri, jnp.zeros((blk, 1), F32), col < row)
    acc = jnp.dot(a.astype(BF16), vblk, preferred_element_type=F32)

    def body(t, state):
        acc, carry = state
        z, vblk = logits(i - 1 - t)
        a, mass = _sb_block(z, tri, carry, None)
        return acc + jnp.dot(a.astype(BF16), vblk, preferred_element_type=F32), carry + mass

    acc, _ = lax.fori_loop(0, i, body, (acc, carry))
    o_ref[...] = acc


def _sb_prompt(proj, sb_bias, tri, n_heads, hd):
    b, t, _ = proj.shape
    blk = min(t, ATTN_BLK)
    return pl.pallas_call(
        functools.partial(_sb_prompt_kernel, blk=blk, scale=1.0 / math.sqrt(hd)),
        out_shape=jax.ShapeDtypeStruct((b, t, n_heads * hd), F32),
        grid=(b, n_heads, t // blk),
        in_specs=[pl.BlockSpec(memory_space=pltpu.SMEM),
                  pl.BlockSpec((None, blk, hd), lambda bi, h, i: (bi, i, h)),
                  pl.BlockSpec((None, t, hd), lambda bi, h, i: (bi, 0, n_heads + h)),
                  pl.BlockSpec((None, t, hd), lambda bi, h, i: (bi, 0, 2 * n_heads + h)),
                  pl.BlockSpec((blk, blk), lambda bi, h, i: (0, 0))],
        out_specs=pl.BlockSpec((None, blk, hd), lambda bi, h, i: (bi, i, h)),
        scratch_shapes=[pltpu.VMEM((t, hd), BF16), pltpu.VMEM((t, hd), BF16)],
        compiler_params=_cparams("parallel", "parallel", "arbitrary"),
        name="sb_prompt",
    )(sb_bias, proj, proj, proj, tri)


def _sb_decode_kernel(pt_ref, q_ref, kn_ref, vn_ref, bias_ref, tri_ref, *rest,
                      n_pages_step, n_heads, scale):
    k_refs = rest[:n_pages_step]
    v_refs = rest[n_pages_step:2 * n_pages_step]
    o_ref = rest[2 * n_pages_step]
    carry_ref, acc_ref = rest[2 * n_pages_step + 1:]
    p = pl.program_id(1)
    page = tri_ref.shape[0]
    rows, hd = q_ref.shape
    slab = rows // n_heads
    tri = tri_ref[...]
    bias = bias_ref[...]
    q = q_ref[...]

    def head_rows(x, h):
        return x[h * slab:(h + 1) * slab]

    def update(k_head, v_head, mask):
        s = jnp.concatenate(
            [lax.dot_general(head_rows(q, h).astype(BF16), k_head(h).astype(BF16), (((1,), (1,)), ((), ())),
                             preferred_element_type=F32) for h in range(n_heads)], axis=0)
        a, mass = _sb_block(s * scale + bias, tri, carry_ref[...], mask)
        acc_ref[...] += jnp.concatenate(
            [jnp.dot(head_rows(a, h).astype(BF16), v_head(h).astype(BF16), preferred_element_type=F32)
             for h in range(n_heads)], axis=0)
        carry_ref[...] += mass

    @pl.when(p == 0)
    def _():
        carry_ref[...] = jnp.zeros_like(carry_ref)
        acc_ref[...] = jnp.zeros_like(acc_ref)
        n_new = kn_ref.shape[0] // n_heads
        pad = jnp.zeros((page - n_new, hd), F32)
        r = lax.broadcasted_iota(jnp.int32, (rows, page), 0)
        c = lax.broadcasted_iota(jnp.int32, (rows, page), 1)
        update(lambda h: jnp.concatenate([kn_ref[pl.ds(h, n_new, stride=n_heads), :], pad], axis=0),
               lambda h: jnp.concatenate([vn_ref[pl.ds(h, n_new, stride=n_heads), :], pad], axis=0),
               c < (r % slab))

    for k_ref, v_ref in zip(k_refs, v_refs):
        update(lambda h: k_ref[pl.ds(h, page, stride=n_heads), :],
               lambda h: v_ref[pl.ds(h, page, stride=n_heads), :], None)

    @pl.when(p == pl.num_programs(1) - 1)
    def _():
        o_ref[...] = acc_ref[...]


def _sb_decode(q_rows, k_new, v_new, cache_k, cache_v, layer, page_table, sb_bias, tri, n_heads):
    b, rows, hd = q_rows.shape
    n_pages = page_table.shape[1]
    page = tri.shape[0]
    pps = PAGES_PER_STEP
    bias_rows = jnp.broadcast_to(jnp.repeat(sb_bias, rows // n_heads)[:, None], (rows, page))

    def page_spec(i):
        return pl.BlockSpec((None, None, page * n_heads, hd),
                            lambda bi, p, pt: (layer, pt[bi, n_pages - 1 - (p * pps + i)], 0, 0))

    grid_spec = pltpu.PrefetchScalarGridSpec(
        num_scalar_prefetch=1,
        grid=(b, n_pages // pps),
        in_specs=[pl.BlockSpec((None, rows, hd), lambda bi, p, pt: (bi, 0, 0)),
                  pl.BlockSpec((None, k_new.shape[1], hd), lambda bi, p, pt: (bi, 0, 0)),
                  pl.BlockSpec((None, v_new.shape[1], hd), lambda bi, p, pt: (bi, 0, 0)),
                  pl.BlockSpec((rows, page), lambda bi, p, pt: (0, 0)),
                  pl.BlockSpec((page, page), lambda bi, p, pt: (0, 0))]
                 + [page_spec(i) for i in range(pps)] * 2,
        out_specs=pl.BlockSpec((None, rows, hd), lambda bi, p, pt: (bi, 0, 0)),
        scratch_shapes=[pltpu.VMEM((rows, page), F32),
                        pltpu.VMEM((rows, hd), F32)],
    )
    return pl.pallas_call(
        functools.partial(_sb_decode_kernel, n_pages_step=pps, n_heads=n_heads, scale=1.0 / math.sqrt(hd)),
        out_shape=jax.ShapeDtypeStruct((b, rows, hd), F32),
        grid_spec=grid_spec,
        compiler_params=_cparams("parallel", "arbitrary"),
        name="sb_decode",
    )(page_table, q_rows, k_new, v_new, bias_rows, tri, *([cache_k] * pps), *([cache_v] * pps))


def _kv_out_kernel(k_ref, v_ref, ko_ref, vo_ref, *, n_heads, hd):
    rows = k_ref.shape[0]
    for h in range(n_heads):
        ko_ref[pl.ds(h, rows, stride=n_heads), :] = k_ref[:, h * hd:(h + 1) * hd]
        vo_ref[pl.ds(h, rows, stride=n_heads), :] = v_ref[:, h * hd:(h + 1) * hd]


def _kv_out(proj, n_heads, hd):
    b, t, _ = proj.shape
    wa = n_heads * hd
    tile = min(t, ROWS_NORM)
    out = jax.ShapeDtypeStruct((b, t * n_heads, hd), F32)
    out_spec = pl.BlockSpec((None, tile * n_heads, hd), lambda bi, i: (bi, i, 0))
    return pl.pallas_call(
        functools.partial(_kv_out_kernel, n_heads=n_heads, hd=hd),
        out_shape=(out, out),
        grid=(b, t // tile),
        in_specs=[pl.BlockSpec((None, tile, wa), lambda bi, i: (bi, i, 1)),
                  pl.BlockSpec((None, tile, wa), lambda bi, i: (bi, i, 2))],
        out_specs=(out_spec, out_spec),
        compiler_params=_cparams("parallel", "parallel"),
        name="kv_out",
    )(proj, proj)


def _gate_kernel(y_ref, z_ref, g_ref, o_ref):
    o_ref[...] = (_rms(y_ref[...], g_ref[...]) * _silu(z_ref[...])).astype(o_ref.dtype)


def _gate(y, proj, z_block, g):
    m, w = y.shape
    tm = min(m, ROWS_NORM)
    return pl.pallas_call(
        _gate_kernel,
        out_shape=jax.ShapeDtypeStruct((m, w), BF16),
        grid=(m // tm,),
        in_specs=[pl.BlockSpec((tm, w), lambda i: (i, 0)),
                  pl.BlockSpec((tm, w), lambda i: (i, z_block)),
                  pl.BlockSpec((1, w), lambda i: (0, 0))],
        out_specs=pl.BlockSpec((tm, w), lambda i: (i, 0)),
        compiler_params=_cparams("parallel"),
        name="gate_a",
    )(y, proj, g)


def _sgu_prompt_kernel(u_ref, v_ref, z_ref, gs_ref, ws_ref, bst_ref, gb_ref, o_ref, *, n_groups, gw):
    chunk = v_ref.shape[0]
    vn = _rms(v_ref[...], gs_ref[...]).astype(BF16)
    row = lax.broadcasted_iota(jnp.int32, (chunk, chunk), 0)
    col = lax.broadcasted_iota(jnp.int32, (chunk, chunk), 1)
    parts = []
    for g in range(n_groups):
        w = jnp.where(col <= row, ws_ref[g], 0.0).astype(BF16)
        mixed = jnp.dot(w, vn[:, g * gw:(g + 1) * gw], preferred_element_type=F32) + bst_ref[:, g:g + 1]
        parts.append(u_ref[:, g * gw:(g + 1) * gw] * mixed)
    yb = jnp.concatenate(parts, axis=1)
    o_ref[...] = (_rms(yb, gb_ref[...]) * _silu(z_ref[...])).astype(o_ref.dtype)


def _sgu_prompt(proj, u_block, g_sgu, w_s, b_s_t, g_b):
    b, t, _ = proj.shape
    n_groups, chunk, _ = w_s.shape
    wb = g_sgu.shape[1]
    assert t % chunk == 0

    def col_spec(blk):
        return pl.BlockSpec((None, chunk, wb), lambda bi, c: (bi, c, blk))

    return pl.pallas_call(
        functools.partial(_sgu_prompt_kernel, n_groups=n_groups, gw=wb // n_groups),
        out_shape=jax.ShapeDtypeStruct((b, t, wb), BF16),
        grid=(b, t // chunk),
        in_specs=[col_spec(u_block), col_spec(u_block + 1), col_spec(u_block + 2),
                  pl.BlockSpec((1, wb), lambda bi, c: (0, 0)),
                  pl.BlockSpec((n_groups, chunk, chunk), lambda bi, c: (0, 0, 0)),
                  pl.BlockSpec((chunk, n_groups), lambda bi, c: (0, 0)),
                  pl.BlockSpec((1, wb), lambda bi, c: (0, 0))],
        out_specs=pl.BlockSpec((None, chunk, wb), lambda bi, c: (bi, c, 0)),
        compiler_params=_cparams("parallel", "parallel"),
        name="sgu_prompt",
    )(proj, proj, proj, g_sgu, w_s, b_s_t, g_b)


def _lru_gates(y, wri_ref, br_ref, bi_ref, lam_ref, n_blocks, bs):
    y16 = y.astype(BF16)
    r_parts, i_parts = [], []
    for n in range(n_blocks):
        ri = jnp.dot(y16[:, n * bs:(n + 1) * bs], wri_ref[n], preferred_element_type=F32)
        r_parts.append(ri[:, :bs])
        i_parts.append(ri[:, bs:])
    r = jax.nn.sigmoid(jnp.concatenate(r_parts, axis=1) + br_ref[...])
    i = jax.nn.sigmoid(jnp.concatenate(i_parts, axis=1) + bi_ref[...])
    log_a = -LRU_C * r * _softplus(-lam_ref[...])
    a = jnp.exp(log_a)
    u = jnp.sqrt(-jnp.tanh(log_a) * (a * a + 1.0)) * (i * y)
    return a, u


def _lru_prompt_kernel(x_ref, z_ref, cw_ref, cb_ref, wri_ref, br_ref, bi_ref, lam_ref, gb_ref,
                       o_ref, h_ref, xp_ref, hc_ref, *, n_blocks, bs, conv_w):
    c = pl.program_id(1)
    tt = x_ref.shape[0]

    @pl.when(c == 0)
    def _():
        xp_ref[0:SUBLANES, :] = jnp.zeros((SUBLANES, xp_ref.shape[1]), F32)
        hc_ref[...] = jnp.zeros_like(hc_ref)

    @pl.when(c > 0)
    def _():
        xp_ref[0:SUBLANES, :] = xp_ref[tt:tt + SUBLANES, :]

    xp_ref[SUBLANES:, :] = x_ref[...]
    y = xp_ref[SUBLANES:, :] * cw_ref[conv_w - 1:conv_w, :] + cb_ref[...]
    for j in range(conv_w - 1):
        shift = conv_w - 1 - j
        y = y + xp_ref[pl.ds(SUBLANES - shift, tt), :] * cw_ref[j:j + 1, :]

    a, u = _lru_gates(y, wri_ref, br_ref, bi_ref, lam_ref, n_blocks, bs)

    row = lax.broadcasted_iota(jnp.int32, a.shape, 0)
    k = 1
    while k < tt:
        a_prev = jnp.where(row >= k, pltpu.roll(a, k, 0), 1.0)
        u_prev = jnp.where(row >= k, pltpu.roll(u, k, 0), 0.0)
        u = a * u_prev + u
        a = a * a_prev
        k *= 2
    h = a * hc_ref[SUBLANES - 1:SUBLANES, :] + u
    hc_ref[...] = h[tt - SUBLANES:, :]
    h_ref[...] = h[tt - SUBLANES:, :]
    o_ref[...] = (_rms(h, gb_ref[...]) * _silu(z_ref[...])).astype(o_ref.dtype)


def _lru_prompt(proj, x_block, conv_w, conv_b, wri, b_r, b_i, lam, g_c):
    b, t, _ = proj.shape
    wc = conv_b.shape[1]
    n_blocks, bs, _ = wri.shape
    tt = min(t, LRU_ROWS)
    cw = conv_w.shape[0]
    vec = pl.BlockSpec((1, wc), lambda bi, c: (0, 0))
    return pl.pallas_call(
        functools.partial(_lru_prompt_kernel, n_blocks=n_blocks, bs=bs, conv_w=cw),
        out_shape=(jax.ShapeDtypeStruct((b, t, wc), BF16),
                   jax.ShapeDtypeStruct((b, SUBLANES, wc), F32)),
        grid=(b, t // tt),
        in_specs=[pl.BlockSpec((None, tt, wc), lambda bi, c: (bi, c, x_block)),
                  pl.BlockSpec((None, tt, wc), lambda bi, c: (bi, c, x_block + 1)),
                  pl.BlockSpec((cw, wc), lambda bi, c: (0, 0)),
                  vec,
                  pl.BlockSpec((n_blocks, bs, 2 * bs), lambda bi, c: (0, 0, 0)),
                  vec, vec, vec, vec],
        out_specs=(pl.BlockSpec((None, tt, wc), lambda bi, c: (bi, c, 0)),
                   pl.BlockSpec((None, SUBLANES, wc), lambda bi, c: (bi, 0, 0))),
        scratch_shapes=[pltpu.VMEM((tt + SUBLANES, wc), F32), pltpu.VMEM((SUBLANES, wc), F32)],
        compiler_params=_cparams("parallel", "arbitrary"),
        name="lru_prompt",
    )(proj, proj, conv_w, conv_b, wri, b_r, b_i, lam, g_c)


def _sample_bc_kernel(u_ref, v_ref, zb_ref, x_ref, zc_ref, gs_ref, wbig_ref, bst_ref, gb_ref,
                      sc_ref, h0_ref, cw_ref, cb_ref, wri_ref, br_ref, bi_ref, lam_ref, gc_ref,
                      ob_ref, oc_ref, vn_ref, h_ref, conv_ref, *, n_groups, gw, n_blocks, bs, conv_w, n_b):
    rows = v_ref.shape[0]
    n_t = rows // n_b
    kpad = wbig_ref.shape[2]
    vn = _rms(v_ref[...], gs_ref[...])
    vn_ref[...] = vn
    vpad = jnp.concatenate([vn, jnp.zeros((kpad - rows, vn.shape[1]), F32)], axis=0).astype(BF16)
    r = lax.broadcasted_iota(jnp.int32, (rows, kpad), 0)
    c = lax.broadcasted_iota(jnp.int32, (rows, kpad), 1)
    keep = (c < rows) & ((c % n_b) == (r % n_b)) & ((c // n_b) <= (r // n_b))
    parts = []
    for g in range(n_groups):
        w = jnp.where(keep, wbig_ref[g], 0.0).astype(BF16)
        mixed = jnp.dot(w, vpad[:, g * gw:(g + 1) * gw], preferred_element_type=F32) + bst_ref[:, g:g + 1]
        parts.append(u_ref[:, g * gw:(g + 1) * gw] * mixed)
    yb = jnp.concatenate(parts, axis=1)
    ob_ref[...] = (_rms(yb, gb_ref[...]) * _silu(zb_ref[...])).astype(ob_ref.dtype)

    full = jnp.concatenate([sc_ref[...], x_ref[...]], axis=0)
    conv_ref[...] = full[rows:, :]
    ys = []
    for t in range(n_t):
        def slab(m):
            return full[m * n_b:(m + 1) * n_b, :]
        y = slab(t + conv_w - 1) * cw_ref[conv_w - 1:conv_w, :] + cb_ref[...]
        for j in range(conv_w - 1):
            y = y + slab(t + j) * cw_ref[j:j + 1, :]
        ys.append(y)
    y = jnp.concatenate(ys, axis=0)
    a, u = _lru_gates(y, wri_ref, br_ref, bi_ref, lam_ref, n_blocks, bs)
    h = h0_ref[...]
    hs = []
    for t in range(n_t):
        h = a[t * n_b:(t + 1) * n_b, :] * h + u[t * n_b:(t + 1) * n_b, :]
        hs.append(h)
    h_ref[...] = h
    yc = jnp.concatenate(hs, axis=0)
    oc_ref[...] = (_rms(yc, gc_ref[...]) * _silu(zc_ref[...])).astype(oc_ref.dtype)


def _sample_bc(proj, u_block, x_block, g_sgu, wbig, bst, g_b, sc, h0, conv_w, conv_b, wri, b_r, b_i, lam, g_c):
    rows = proj.shape[0]
    wb = g_sgu.shape[1]
    wc = conv_b.shape[1]
    n_b = h0.shape[0]
    n_groups = wbig.shape[0]
    n_blocks, bs, _ = wri.shape
    cw = conv_w.shape[0]

    def full(shape):
        return pl.BlockSpec(shape, lambda i: (0,) * len(shape))

    return pl.pallas_call(
        functools.partial(_sample_bc_kernel, n_groups=n_groups, gw=wb // n_groups, n_blocks=n_blocks,
                          bs=bs, conv_w=cw, n_b=n_b),
        out_shape=(jax.ShapeDtypeStruct((rows, wb), BF16),
                   jax.ShapeDtypeStruct((rows, wc), BF16),
                   jax.ShapeDtypeStruct((rows, wb), F32),
                   jax.ShapeDtypeStruct((n_b, wc), F32),
                   jax.ShapeDtypeStruct(sc.shape, F32)),
        grid=(1,),
        in_specs=[pl.BlockSpec((rows, wb), lambda i: (0, u_block)),
                  pl.BlockSpec((rows, wb), lambda i: (0, u_block + 1)),
                  pl.BlockSpec((rows, wb), lambda i: (0, u_block + 2)),
                  pl.BlockSpec((rows, wc), lambda i: (0, x_block)),
                  pl.BlockSpec((rows, wc), lambda i: (0, x_block + 1)),
                  full((1, wb)), full(wbig.shape), full(bst.shape), full((1, wb)),
                  full(sc.shape), full(h0.shape), full(conv_w.shape), full((1, wc)),
                  full(wri.shape), full((1, wc)), full((1, wc)), full((1, wc)), full((1, wc))],
        out_specs=(full((rows, wb)), full((rows, wc)), full((rows, wb)), full((n_b, wc)), full(sc.shape)),
        compiler_params=_cparams("arbitrary"),
        name="sample_bc",
    )(proj, proj, proj, proj, proj, g_sgu, wbig, bst, g_b, sc, h0, conv_w, conv_b, wri, b_r, b_i, lam, g_c)


def _cross_kernel(x_ref, g_ref, wq_ref, mk_ref, mv_ref, wo_ref, gf_ref, o_ref, *, n_heads, hd, n_b, final):
    x = x_ref[...]
    hn = _rms(x, g_ref[...]).astype(BF16)
    q = jnp.dot(hn, wq_ref[...].astype(BF16), preferred_element_type=F32)
    n_mem_rows = mk_ref.shape[0]
    rows = x.shape[0]
    if n_b:
        r = lax.broadcasted_iota(jnp.int32, (rows, n_mem_rows), 0)
        c = lax.broadcasted_iota(jnp.int32, (rows, n_mem_rows), 1)
        own = (c // (n_mem_rows // n_b)) == (r % n_b)
    outs = []
    for h in range(n_heads):
        qh = q[:, h * hd:(h + 1) * hd].astype(BF16)
        kh = mk_ref[:, h * hd:(h + 1) * hd].astype(BF16)
        vh = mv_ref[:, h * hd:(h + 1) * hd].astype(BF16)
        s = lax.dot_general(qh, kh, (((1,), (1,)), ((), ())), preferred_element_type=F32)
        s = s * (1.0 / math.sqrt(hd))
        if n_b:
            s = jnp.where(own, s, -jnp.inf)
        e = jnp.exp(s - jnp.max(s, axis=-1, keepdims=True))
        p = e / jnp.sum(e, axis=-1, keepdims=True)
        outs.append(jnp.dot(p.astype(BF16), vh, preferred_element_type=F32))
    o = jnp.concatenate(outs, axis=1).astype(BF16)
    y = x + jnp.dot(o, wo_ref[...].astype(BF16), preferred_element_type=F32)
    if final:
        y = _rms(y, gf_ref[...])
    o_ref[...] = y


def _cross(x, g, w_cq, mk, mv, w_co, layer, g_final, final, rows_per_batch, n_heads, hd):
    m, d = x.shape
    wx = n_heads * hd
    if rows_per_batch:
        tm = min(rows_per_batch, CROSS_ROWS)
        tiles_per_batch = rows_per_batch // tm
        n_mem = mk.shape[0] // (m // rows_per_batch)
        mem_spec = pl.BlockSpec((n_mem, wx), lambda i: (i // tiles_per_batch, 0))
        n_b = 0
    else:
        tm = m
        mem_spec = pl.BlockSpec(mk.shape, lambda i: (0, 0))
        n_b = SUBLANES
    return pl.pallas_call(
        functools.partial(_cross_kernel, n_heads=n_heads, hd=hd, n_b=n_b, final=final),
        out_shape=jax.ShapeDtypeStruct((m, d), F32),
        grid=(m // tm,),
        in_specs=[pl.BlockSpec((tm, d), lambda i: (i, 0)),
                  pl.BlockSpec((1, d), lambda i: (0, 0)),
                  pl.BlockSpec((None, d, wx), lambda i: (layer, 0, 0)),
                  mem_spec, mem_spec,
                  pl.BlockSpec((None, wx, d), lambda i: (layer, 0, 0)),
                  pl.BlockSpec((1, d), lambda i: (0, 0))],
        out_specs=pl.BlockSpec((tm, d), lambda i: (i, 0)),
        compiler_params=_cparams("parallel"),
        name="cross",
    )(x, g, w_cq, mk, mv, w_co, g_final)


def kernel(x_prompt, x_sample, mem_prompt, cache_k, cache_v, page_table, cache_mem_k, cache_mem_v, state_h, state_conv, g_mix, w_in, g_branch, w_out, g_sgu, w_s, b_s, conv_w, conv_b, w_r, b_r, w_i, b_i, lru_lambda, sb_bias, g_cross, w_cq, w_ck, w_cv, w_co, g_final):
    bp, seq, d = x_prompt.shape
    db, dt, _ = x_sample.shape
    depth, n_pool, page, n_heads, hd = cache_k.shape
    n_mem, hx, hdx = cache_mem_k.shape[2:]
    wa = n_heads * hd
    wb = g_sgu.shape[1]
    wc = state_h.shape[2]
    wx = hx * hdx
    n_groups, chunk = w_s.shape[1], w_s.shape[2]
    n_blocks, bs = w_r.shape[1], w_r.shape[2]
    cw = conv_w.shape[1]
    n_in = w_in.shape[2]
    assert n_in == 4 * wa + 3 * wb + 2 * wc and (4 * wa) % wb == 0 and (4 * wa + 3 * wb) % wc == 0
    assert db == SUBLANES and dt <= SUBLANES and dt <= chunk and cw - 1 <= SUBLANES
    u_block = 4 * wa // wb
    x_block = (4 * wa + 3 * wb) // wc
    z_block = 3

    cache_k2 = cache_k.reshape(depth, n_pool, page * n_heads, hd)
    cache_v2 = cache_v.reshape(depth, n_pool, page * n_heads, hd)
    mem_x = mem_prompt.reshape(bp * n_mem, d)
    w_ckv = jnp.concatenate([w_ck, w_cv], axis=2)
    w_cq = w_cq.astype(BF16)
    w_co = w_co.astype(BF16)
    wri = jnp.concatenate([w_r, w_i], axis=3).astype(BF16)
    g_fin = g_final.reshape(1, d)
    idx = jnp.arange(page)
    tri_page = (idx[:, None] >= idx[None, :]).astype(BF16)
    blk = min(seq, ATTN_BLK)
    idx = jnp.arange(blk)
    tri_blk = (idx[:, None] >= idx[None, :]).astype(BF16)

    xp = x_prompt.reshape(bp * seq, d)
    xs = jnp.swapaxes(x_sample, 0, 1).reshape(dt * db, d)

    outs = {k: [] for k in ("kp", "vp", "ks", "vs", "mk", "mv", "hp", "hs", "cp", "cs", "sgu")}
    for l in range(depth):
        gm = g_mix[l].reshape(1, d)
        g_a = g_branch[l, :wa].reshape(1, wa)
        g_b = g_branch[l, wa:wa + wb].reshape(1, wb)
        g_c = g_branch[l, wa + wb:].reshape(1, wc)
        gs = g_sgu[l].reshape(1, wb)
        cb = conv_b[l].reshape(1, wc)
        br = b_r[l].reshape(1, wc)
        bi = b_i[l].reshape(1, wc)
        lam = lru_lambda[l].reshape(1, wc)
        gc = g_cross[l].reshape(1, d)
        final = l == depth - 1
        w_out_l = [(w_out, l, 0), (w_out, l, wa // wb), (w_out, l, (wa + wb) // wc)]

        hn = _rmsnorm(xp, gm, BF16)
        proj = _matmul([hn], [(w_in, l, 0)], n_in, name="in_proj")
        proj3 = proj.reshape(bp, seq, n_in)
        ya = _sb_prompt(proj3, sb_bias[l], tri_blk, n_heads, hd)
        cat_a = _gate(ya.reshape(bp * seq, wa), proj, z_block, g_a)
        cat_b = _sgu_prompt(proj3, u_block, gs, w_s[l], jnp.transpose(b_s[l]), g_b)
        cat_c, h_p = _lru_prompt(proj3, x_block, conv_w[l], cb, wri[l], br, bi, lam, g_c)
        x1 = _matmul([cat_a, cat_b.reshape(bp * seq, wb), cat_c.reshape(bp * seq, wc)], w_out_l, d,
                     res=xp, name="out_proj")
        mkv = _matmul([mem_x], [(w_ckv, l, 0)], 2 * wx, rows=ROWS_NORM, name="mem_proj")
        mk, mv = mkv[:, :wx], mkv[:, wx:]
        xp = _cross(x1, gc, w_cq, mk, mv, w_co, l, g_fin, final, seq, hx, hdx)
        k_p, v_p = _kv_out(proj3, n_heads, hd)
        outs["kp"].append(k_p.reshape(bp, seq, n_heads, hd))
        outs["vp"].append(v_p.reshape(bp, seq, n_heads, hd))
        outs["mk"].append(mk.reshape(bp, n_mem, hx, hdx))
        outs["mv"].append(mv.reshape(bp, n_mem, hx, hdx))
        outs["hp"].append(h_p[:, SUBLANES - 1, :])
        outs["cp"].append(proj3[:, seq - (cw - 1):, x_block * wc:(x_block + 1) * wc])

        hn_s = _rmsnorm(xs, gm, BF16)
        proj_s = _matmul([hn_s], [(w_in, l, 0)], n_in, name="in_proj_s")

        def batch_major(a):
            return jnp.swapaxes(a.reshape(dt, db, -1), 0, 1)

        q_s = batch_major(proj_s[:, :wa]).reshape(db, dt, n_heads, hd)
        k_s = batch_major(proj_s[:, wa:2 * wa])
        v_s = batch_major(proj_s[:, 2 * wa:3 * wa])
        pad_t = ((0, 0), (0, 0), (0, SUBLANES - dt), (0, 0))
        q_rows = jnp.pad(jnp.swapaxes(q_s, 1, 2), pad_t).reshape(db, n_heads * SUBLANES, hd)
        pad_j = ((0, 0), (0, (SUBLANES - dt) * n_heads), (0, 0))
        ya_s = _sb_decode(q_rows, jnp.pad(k_s.reshape(db, dt * n_heads, hd), pad_j),
                          jnp.pad(v_s.reshape(db, dt * n_heads, hd), pad_j), cache_k2, cache_v2, l, page_table,
                          sb_bias[l], tri_page, n_heads)
        ya_s = ya_s.reshape(db, n_heads, SUBLANES, hd)[:, :, :dt]
        ya_s = jnp.transpose(ya_s, (2, 0, 1, 3)).reshape(dt * db, wa)
        cat_a_s = _gate(ya_s, proj_s, z_block, g_a)
        wbig = jnp.repeat(jnp.repeat(w_s[l][:, :dt, :dt], db, axis=1), db, axis=2)
        wbig = jnp.pad(wbig, ((0, 0), (0, 0), (0, LANES - dt * db)))
        bst = jnp.repeat(jnp.transpose(b_s[l][:, :dt]), db, axis=0)
        sc = jnp.swapaxes(state_conv[l], 0, 1).reshape((cw - 1) * db, wc)
        cat_b_s, cat_c_s, vn_s, h_s, conv_s = _sample_bc(
            proj_s, u_block, x_block, gs, wbig, bst, g_b, sc, state_h[l], conv_w[l], cb, wri[l], br, bi, lam, g_c)
        x1_s = _matmul([cat_a_s, cat_b_s, cat_c_s], w_out_l, d, res=xs, name="out_proj_s")
        xs = _cross(x1_s, gc, w_cq, cache_mem_k[l].reshape(db * n_mem, wx), cache_mem_v[l].reshape(db * n_mem, wx),
                    w_co, l, g_fin, final, 0, hx, hdx)
        outs["ks"].append(k_s.reshape(db, dt, n_heads, hd))
        outs["vs"].append(v_s.reshape(db, dt, n_heads, hd))
        outs["hs"].append(h_s)
        outs["cs"].append(jnp.swapaxes(conv_s.reshape(cw - 1, db, wc), 0, 1))
        outs["sgu"].append(batch_major(vn_s))

    y_prompt = xp.reshape(bp, seq, d)
    y_sample = jnp.swapaxes(xs.reshape(dt, db, d), 0, 1)
    st = {k: jnp.stack(v) for k, v in outs.items()}
    return (y_prompt, y_sample, st["kp"], st["vp"], st["ks"], st["vs"], st["mk"], st["mv"],
            st["hp"], st["hs"], st["cp"], st["cs"], st["sgu"])
```

```python
import functools
import math

import jax
import jax.numpy as jnp
from jax import lax
from jax.experimental import pallas as pl
from jax.experimental.pallas import tpu as pltpu

F32 = jnp.float32
BF16 = jnp.bfloat16

EPS = 1e-6
LRU_C = 8.0
SUBLANES = 8
LANES = 128
VMEM_LIMIT_BYTES = 56 * 1024 * 1024

ROWS_NORM = 256
ROWS_MM = 1024
COLS_MM = 512
ATTN_BLK = 256
ATTN_HEADS = 4
LRU_ROWS = 256
CROSS_ROWS = 256
PAGES_PER_STEP = 4


def _cparams(*sem):
    return pltpu.CompilerParams(dimension_semantics=sem, vmem_limit_bytes=VMEM_LIMIT_BYTES)


def _softplus(x):
    return jnp.maximum(x, 0.0) + jnp.log1p(jnp.exp(-jnp.abs(x)))


def _softplus_sum(x):
    return jnp.maximum(x, 0.0) + jnp.log(1.0 + jnp.exp(-jnp.abs(x)))


def _rms(x, g):
    return x * lax.rsqrt(jnp.mean(x * x, axis=-1, keepdims=True) + EPS) * g


def _silu(x):
    return x * jax.nn.sigmoid(x)


def _split_bf16(x):
    hi = x.astype(BF16)
    lo = (x - hi.astype(F32)).astype(BF16)
    return hi, lo


def _rmsnorm_kernel(x_ref, g_ref, o_ref):
    o_ref[...] = _rms(x_ref[...], g_ref[...]).astype(o_ref.dtype)


def _rmsnorm(x, g, out_dtype):
    m, d = x.shape
    tm = min(m, ROWS_NORM)
    return pl.pallas_call(
        _rmsnorm_kernel,
        out_shape=jax.ShapeDtypeStruct((m, d), out_dtype),
        grid=(m // tm,),
        in_specs=[pl.BlockSpec((tm, d), lambda i: (i, 0)),
                  pl.BlockSpec((1, d), lambda i: (0, 0))],
        out_specs=pl.BlockSpec((tm, d), lambda i: (i, 0)),
        compiler_params=_cparams("parallel"),
        name="rmsnorm",
    )(x, g.reshape(1, d))


def _matmul_kernel(*refs, n_in, has_res):
    o_ref = refs[-1]
    acc = None
    for a_ref, w_ref in zip(refs[:n_in], refs[n_in:2 * n_in]):
        d = jnp.dot(a_ref[...].astype(BF16), w_ref[...].astype(BF16), preferred_element_type=F32)
        acc = d if acc is None else acc + d
    if has_res:
        acc = acc + refs[2 * n_in][...]
    o_ref[...] = acc


def _matmul(a_list, w_list, n, res=None, rows=ROWS_MM, name="matmul"):
    m = a_list[0].shape[0]
    tm = min(m, rows)
    tn = min(n, COLS_MM)
    in_specs, args = [], []
    for a in a_list:
        in_specs.append(pl.BlockSpec((tm, a.shape[1]), lambda i, j: (i, 0)))
        args.append(a)
    for a, (w, layer, rb) in zip(a_list, w_list):
        in_specs.append(pl.BlockSpec((None, a.shape[1], tn),
                                     functools.partial(lambda i, j, layer, rb: (layer, rb, j), layer=layer, rb=rb)))
        args.append(w)
    if res is not None:
        in_specs.append(pl.BlockSpec((tm, tn), lambda i, j: (i, j)))
        args.append(res)
    return pl.pallas_call(
        functools.partial(_matmul_kernel, n_in=len(a_list), has_res=res is not None),
        out_shape=jax.ShapeDtypeStruct((m, n), F32),
        grid=(m // tm, n // tn),
        in_specs=in_specs,
        out_specs=pl.BlockSpec((tm, tn), lambda i, j: (i, j)),
        compiler_params=_cparams("parallel", "arbitrary"),
        name=name,
    )(*args)


MASKED_LOGIT = -1e30


def _sb_tail(z, tri):
    hi, lo = _split_bf16(_softplus_sum(z))
    return jnp.dot(hi, tri, preferred_element_type=F32) + jnp.dot(lo, tri, preferred_element_type=F32)


def _sb_prompt_kernel(bias_ref, q_ref, k_ref, v_ref, tri_ref, o_ref, kb_ref, vb_ref, z_ref, tail_ref,
                      *, blk, hd, scale):
    hg = pl.program_id(1)
    i = pl.program_id(2)
    n_h = q_ref.shape[1] // hd

    @pl.when(i == 0)
    def _():
        kb_ref[...] = k_ref[...].astype(BF16)
        vb_ref[...] = v_ref[...].astype(BF16)

    q = [q_ref[:, h * hd:(h + 1) * hd].astype(BF16) for h in range(n_h)]
    bias = [bias_ref[hg * n_h + h] for h in range(n_h)]
    tri = tri_ref[...]

    def rows(j):
        return pl.ds(pl.multiple_of(j * blk, blk), blk)

    def front(j, h, mask):
        kblk = kb_ref[rows(j), h * hd:(h + 1) * hd]
        z = lax.dot_general(q[h], kblk, (((1,), (1,)), ((), ())), preferred_element_type=F32) * scale + bias[h]
        if mask is not None:
            z = jnp.where(mask, z, MASKED_LOGIT)
        z_ref[h] = z
        tail_ref[h] = _sb_tail(z, tri)

    def back(j, h, acc, carry):
        tail = tail_ref[h]
        a = jnp.exp(z_ref[h] - tail - carry)
        pv = jnp.dot(a.astype(BF16), vb_ref[rows(j), h * hd:(h + 1) * hd], preferred_element_type=F32)
        return acc + pv, carry + tail[:, 0:1]

    row = lax.broadcasted_iota(jnp.int32, (blk, blk), 0)
    col = lax.broadcasted_iota(jnp.int32, (blk, blk), 1)
    for h in range(n_h):
        front(i, h, col < row)

    def body(t, state):
        j = i - 1 - t
        state = tuple(back(j + 1, h, *state[h]) for h in range(n_h))
        for h in range(n_h):
            front(j, h, None)
        return state

    zero = (jnp.zeros((blk, hd), F32), jnp.zeros((blk, 1), F32))
    state = lax.fori_loop(0, i, body, (zero,) * n_h)
    for h in range(n_h):
        acc, _ = back(0, h, *state[h])
        o_ref[:, h * hd:(h + 1) * hd] = acc


def _sb_prompt(proj, sb_bias, tri, n_heads, hd):
    b, t, _ = proj.shape
    blk = min(t, ATTN_BLK)
    n_h = ATTN_HEADS
    groups = n_heads // n_h
    w = n_h * hd
    return pl.pallas_call(
        functools.partial(_sb_prompt_kernel, blk=blk, hd=hd, scale=1.0 / math.sqrt(hd)),
        out_shape=jax.ShapeDtypeStruct((b, t, n_heads * hd), F32),
        grid=(b, groups, t // blk),
        in_specs=[pl.BlockSpec(memory_space=pltpu.SMEM),
                  pl.BlockSpec((None, blk, w), lambda bi, hg, i: (bi, i, hg)),
                  pl.BlockSpec((None, t, w), lambda bi, hg, i: (bi, 0, groups + hg)),
                  pl.BlockSpec((None, t, w), lambda bi, hg, i: (bi, 0, 2 * groups + hg)),
                  pl.BlockSpec((blk, blk), lambda bi, hg, i: (0, 0))],
        out_specs=pl.BlockSpec((None, blk, w), lambda bi, hg, i: (bi, i, hg)),
        scratch_shapes=[pltpu.VMEM((t, w), BF16), pltpu.VMEM((t, w), BF16),
                        pltpu.VMEM((n_h, blk, blk), F32), pltpu.VMEM((n_h, blk, blk), F32)],
        compiler_params=_cparams("parallel", "parallel", "arbitrary"),
        name="sb_prompt",
    )(sb_bias, proj, proj, proj, tri)


def _sb_decode_kernel(pt_ref, q_ref, kn_ref, vn_ref, bias_ref, tri_ref, *rest,
                      n_pages_step, n_heads, scale):
    n_blocks = n_pages_step
    k_refs = rest[:n_blocks]
    v_refs = rest[n_blocks:2 * n_blocks]
    o_ref = rest[2 * n_blocks]
    carry_ref, acc_ref = rest[2 * n_blocks + 1:]
    p = pl.program_id(1)
    page = tri_ref.shape[0]
    rows, hd = q_ref.shape
    slab = rows // n_heads
    tri = tri_ref[...]
    bias = bias_ref[...]
    q = q_ref[...]

    def head_rows(x, h):
        return x[h * slab:(h + 1) * slab]

    def front(k_head, mask):
        s = jnp.concatenate(
            [lax.dot_general(head_rows(q, h).astype(BF16), k_head(h).astype(BF16), (((1,), (1,)), ((), ())),
                             preferred_element_type=F32) for h in range(n_heads)], axis=0)
        z = s * scale + bias
        if mask is not None:
            z = jnp.where(mask, z, MASKED_LOGIT)
        return z, _sb_tail(z, tri)

    def back(z, tail, carry, v_head):
        a = jnp.exp(z - tail - carry)
        pv = jnp.concatenate(
            [jnp.dot(head_rows(a, h).astype(BF16), v_head(h).astype(BF16), preferred_element_type=F32)
             for h in range(n_heads)], axis=0)
        return pv, carry + tail[:, 0:1]

    @pl.when(p == 0)
    def _():
        n_new = kn_ref.shape[0] // n_heads
        pad = jnp.zeros((page - n_new, hd), F32)
        r = lax.broadcasted_iota(jnp.int32, (rows, page), 0)
        c = lax.broadcasted_iota(jnp.int32, (rows, page), 1)
        z, tail = front(lambda h: jnp.concatenate([kn_ref[pl.ds(h, n_new, stride=n_heads), :], pad], axis=0),
                        c < (r % slab))
        pv, carry = back(z, tail, jnp.zeros((rows, page), F32),
                         lambda h: jnp.concatenate([vn_ref[pl.ds(h, n_new, stride=n_heads), :], pad], axis=0))
        acc_ref[...] = pv
        carry_ref[...] = carry

    def head_of(refs, i, h):
        return refs[i][pl.ds(h, page, stride=n_heads), :]

    fronts = [front(lambda h: head_of(k_refs, i, h), None) for i in range(n_pages_step)]
    carry = carry_ref[...]
    acc = acc_ref[...]
    for i, (z, tail) in enumerate(fronts):
        pv, carry = back(z, tail, carry, lambda h: head_of(v_refs, i, h))
        acc = acc + pv
    carry_ref[...] = carry
    acc_ref[...] = acc

    @pl.when(p == pl.num_programs(1) - 1)
    def _():
        o_ref[...] = acc_ref[...]


def _sb_decode(q_rows, k_new, v_new, cache_k, cache_v, layer, page_table, sb_bias, tri, n_heads):
    b, rows, hd = q_rows.shape
    n_pages = page_table.shape[1]
    page = tri.shape[0]
    pps = PAGES_PER_STEP
    bias_rows = jnp.broadcast_to(jnp.repeat(sb_bias, rows // n_heads)[:, None], (rows, page))

    def page_spec(i):
        return pl.BlockSpec((None, None, page * n_heads, hd),
                            lambda bi, p, pt: (layer, pt[bi, n_pages - 1 - (p * pps + i)], 0, 0))

    grid_spec = pltpu.PrefetchScalarGridSpec(
        num_scalar_prefetch=1,
        grid=(b, n_pages // pps),
        in_specs=[pl.BlockSpec((None, rows, hd), lambda bi, p, pt: (bi, 0, 0)),
                  pl.BlockSpec((None, k_new.shape[1], hd), lambda bi, p, pt: (bi, 0, 0)),
                  pl.BlockSpec((None, v_new.shape[1], hd), lambda bi, p, pt: (bi, 0, 0)),
                  pl.BlockSpec((rows, page), lambda bi, p, pt: (0, 0)),
                  pl.BlockSpec((page, page), lambda bi, p, pt: (0, 0))]
                 + [page_spec(i) for i in range(pps)] * 2,
        out_specs=pl.BlockSpec((None, rows, hd), lambda bi, p, pt: (bi, 0, 0)),
        scratch_shapes=[pltpu.VMEM((rows, page), F32),
                        pltpu.VMEM((rows, hd), F32)],
    )
    return pl.pallas_call(
        functools.partial(_sb_decode_kernel, n_pages_step=pps, n_heads=n_heads, scale=1.0 / math.sqrt(hd)),
        out_shape=jax.ShapeDtypeStruct((b, rows, hd), F32),
        grid_spec=grid_spec,
        compiler_params=_cparams("parallel", "arbitrary"),
        name="sb_decode",
    )(page_table, q_rows, k_new, v_new, bias_rows, tri, *([cache_k] * pps), *([cache_v] * pps))


def _kv_out_kernel(*refs, n_layers, n_heads, hd):
    ko_ref, vo_ref = refs[2 * n_layers:]
    layer = pl.program_id(0)
    for l in range(n_layers):
        @pl.when(layer == l)
        def _():
            k_ref, v_ref = refs[2 * l], refs[2 * l + 1]
            rows = k_ref.shape[0]
            for h in range(n_heads):
                ko_ref[pl.ds(h, rows, stride=n_heads), :] = k_ref[:, h * hd:(h + 1) * hd]
                vo_ref[pl.ds(h, rows, stride=n_heads), :] = v_ref[:, h * hd:(h + 1) * hd]


def _kv_out(projs, n_heads, hd):
    n_layers = len(projs)
    b, t, _ = projs[0].shape
    wa = n_heads * hd
    tile = min(t, ROWS_NORM)
    n_tiles = t // tile
    out = jax.ShapeDtypeStruct((n_layers, b, t * n_heads, hd), F32)
    out_spec = pl.BlockSpec((None, None, tile * n_heads, hd), lambda l, bi, i: (l, bi, i, 0))

    def in_spec(layer, col):
        def index(l, bi, i):
            before, own = l < layer, l == layer
            return (jnp.where(own, bi, jnp.where(before, 0, b - 1)),
                    jnp.where(own, i, jnp.where(before, 0, n_tiles - 1)), col)
        return pl.BlockSpec((None, tile, wa), index)

    in_specs, args = [], []
    for l, proj in enumerate(projs):
        in_specs += [in_spec(l, 1), in_spec(l, 2)]
        args += [proj, proj]
    return pl.pallas_call(
        functools.partial(_kv_out_kernel, n_layers=n_layers, n_heads=n_heads, hd=hd),
        out_shape=(out, out),
        grid=(n_layers, b, n_tiles),
        in_specs=in_specs,
        out_specs=(out_spec, out_spec),
        compiler_params=_cparams("arbitrary", "arbitrary", "arbitrary"),
        name="kv_out",
    )(*args)


def _gate_kernel(y_ref, z_ref, g_ref, o_ref):
    o_ref[...] = (_rms(y_ref[...], g_ref[...]) * _silu(z_ref[...])).astype(o_ref.dtype)


def _gate(y, proj, z_block, g):
    m, w = y.shape
    tm = min(m, ROWS_NORM)
    return pl.pallas_call(
        _gate_kernel,
        out_shape=jax.ShapeDtypeStruct((m, w), BF16),
        grid=(m // tm,),
        in_specs=[pl.BlockSpec((tm, w), lambda i: (i, 0)),
                  pl.BlockSpec((tm, w), lambda i: (i, z_block)),
                  pl.BlockSpec((1, w), lambda i: (0, 0))],
        out_specs=pl.BlockSpec((tm, w), lambda i: (i, 0)),
        compiler_params=_cparams("parallel"),
        name="gate_a",
    )(y, proj, g)


def _sgu_prompt_kernel(u_ref, v_ref, z_ref, gs_ref, ws_ref, bst_ref, gb_ref, o_ref, *, n_groups, gw):
    chunk = v_ref.shape[0]
    vn = _rms(v_ref[...], gs_ref[...]).astype(BF16)
    row = lax.broadcasted_iota(jnp.int32, (chunk, chunk), 0)
    col = lax.broadcasted_iota(jnp.int32, (chunk, chunk), 1)
    parts = []
    for g in range(n_groups):
        w = jnp.where(col <= row, ws_ref[g], 0.0).astype(BF16)
        mixed = jnp.dot(w, vn[:, g * gw:(g + 1) * gw], preferred_element_type=F32) + bst_ref[:, g:g + 1]
        parts.append(u_ref[:, g * gw:(g + 1) * gw] * mixed)
    yb = jnp.concatenate(parts, axis=1)
    o_ref[...] = (_rms(yb, gb_ref[...]) * _silu(z_ref[...])).astype(o_ref.dtype)


def _sgu_prompt(proj, u_block, g_sgu, w_s, b_s_t, g_b):
    b, t, _ = proj.shape
    n_groups, chunk, _ = w_s.shape
    wb = g_sgu.shape[1]
    assert t % chunk == 0

    def col_spec(blk):
        return pl.BlockSpec((None, chunk, wb), lambda bi, c: (bi, c, blk))

    return pl.pallas_call(
        functools.partial(_sgu_prompt_kernel, n_groups=n_groups, gw=wb // n_groups),
        out_shape=jax.ShapeDtypeStruct((b, t, wb), BF16),
        grid=(b, t // chunk),
        in_specs=[col_spec(u_block), col_spec(u_block + 1), col_spec(u_block + 2),
                  pl.BlockSpec((1, wb), lambda bi, c: (0, 0)),
                  pl.BlockSpec((n_groups, chunk, chunk), lambda bi, c: (0, 0, 0)),
                  pl.BlockSpec((chunk, n_groups), lambda bi, c: (0, 0)),
                  pl.BlockSpec((1, wb), lambda bi, c: (0, 0))],
        out_specs=pl.BlockSpec((None, chunk, wb), lambda bi, c: (bi, c, 0)),
        compiler_params=_cparams("parallel", "parallel"),
        name="sgu_prompt",
    )(proj, proj, proj, g_sgu, w_s, b_s_t, g_b)


def _lru_gates(y, wri_ref, br_ref, bi_ref, lam_ref, n_blocks, bs):
    y16 = y.astype(BF16)
    r_parts, i_parts = [], []
    for n in range(n_blocks):
        ri = jnp.dot(y16[:, n * bs:(n + 1) * bs], wri_ref[n], preferred_element_type=F32)
        r_parts.append(ri[:, :bs])
        i_parts.append(ri[:, bs:])
    r = jax.nn.sigmoid(jnp.concatenate(r_parts, axis=1) + br_ref[...])
    i = jax.nn.sigmoid(jnp.concatenate(i_parts, axis=1) + bi_ref[...])
    log_a = -LRU_C * r * _softplus(-lam_ref[...])
    a = jnp.exp(log_a)
    u = jnp.sqrt(-jnp.tanh(log_a) * (a * a + 1.0)) * (i * y)
    return a, u


def _lru_prompt_kernel(x_ref, z_ref, cw_ref, cb_ref, wri_ref, br_ref, bi_ref, lam_ref, gb_ref,
                       o_ref, h_ref, xp_ref, hc_ref, *, n_blocks, bs, conv_w):
    c = pl.program_id(1)
    tt = x_ref.shape[0]

    @pl.when(c == 0)
    def _():
        xp_ref[0:SUBLANES, :] = jnp.zeros((SUBLANES, xp_ref.shape[1]), F32)
        hc_ref[...] = jnp.zeros_like(hc_ref)

    @pl.when(c > 0)
    def _():
        xp_ref[0:SUBLANES, :] = xp_ref[tt:tt + SUBLANES, :]

    xp_ref[SUBLANES:, :] = x_ref[...]
    y = xp_ref[SUBLANES:, :] * cw_ref[conv_w - 1:conv_w, :] + cb_ref[...]
    for j in range(conv_w - 1):
        shift = conv_w - 1 - j
        y = y + xp_ref[pl.ds(SUBLANES - shift, tt), :] * cw_ref[j:j + 1, :]

    a, u = _lru_gates(y, wri_ref, br_ref, bi_ref, lam_ref, n_blocks, bs)

    row = lax.broadcasted_iota(jnp.int32, a.shape, 0)
    k = 1
    while k < tt:
        a_prev = jnp.where(row >= k, pltpu.roll(a, k, 0), 1.0)
        u_prev = jnp.where(row >= k, pltpu.roll(u, k, 0), 0.0)
        u = a * u_prev + u
        a = a * a_prev
        k *= 2
    h = a * hc_ref[SUBLANES - 1:SUBLANES, :] + u
    hc_ref[...] = h[tt - SUBLANES:, :]
    h_ref[...] = h[tt - SUBLANES:, :]
    o_ref[...] = (_rms(h, gb_ref[...]) * _silu(z_ref[...])).astype(o_ref.dtype)


def _lru_prompt(proj, x_block, conv_w, conv_b, wri, b_r, b_i, lam, g_c):
    b, t, _ = proj.shape
    wc = conv_b.shape[1]
    n_blocks, bs, _ = wri.shape
    tt = min(t, LRU_ROWS)
    cw = conv_w.shape[0]
    vec = pl.BlockSpec((1, wc), lambda bi, c: (0, 0))
    return pl.pallas_call(
        functools.partial(_lru_prompt_kernel, n_blocks=n_blocks, bs=bs, conv_w=cw),
        out_shape=(jax.ShapeDtypeStruct((b, t, wc), BF16),
                   jax.ShapeDtypeStruct((b, SUBLANES, wc), F32)),
        grid=(b, t // tt),
        in_specs=[pl.BlockSpec((None, tt, wc), lambda bi, c: (bi, c, x_block)),
                  pl.BlockSpec((None, tt, wc), lambda bi, c: (bi, c, x_block + 1)),
                  pl.BlockSpec((cw, wc), lambda bi, c: (0, 0)),
                  vec,
                  pl.BlockSpec((n_blocks, bs, 2 * bs), lambda bi, c: (0, 0, 0)),
                  vec, vec, vec, vec],
        out_specs=(pl.BlockSpec((None, tt, wc), lambda bi, c: (bi, c, 0)),
                   pl.BlockSpec((None, SUBLANES, wc), lambda bi, c: (bi, 0, 0))),
        scratch_shapes=[pltpu.VMEM((tt + SUBLANES, wc), F32), pltpu.VMEM((SUBLANES, wc), F32)],
        compiler_params=_cparams("parallel", "arbitrary"),
        name="lru_prompt",
    )(proj, proj, conv_w, conv_b, wri, b_r, b_i, lam, g_c)


def _sample_bc_kernel(u_ref, v_ref, zb_ref, x_ref, zc_ref, gs_ref, wbig_ref, bst_ref, gb_ref,
                      sc_ref, h0_ref, cw_ref, cb_ref, wri_ref, br_ref, bi_ref, lam_ref, gc_ref,
                      ob_ref, oc_ref, vn_ref, h_ref, conv_ref, *, n_groups, gw, n_blocks, bs, conv_w, n_b):
    rows = v_ref.shape[0]
    n_t = rows // n_b
    kpad = wbig_ref.shape[2]
    vn = _rms(v_ref[...], gs_ref[...])
    vn_ref[...] = vn
    vpad = jnp.concatenate([vn, jnp.zeros((kpad - rows, vn.shape[1]), F32)], axis=0).astype(BF16)
    r = lax.broadcasted_iota(jnp.int32, (rows, kpad), 0)
    c = lax.broadcasted_iota(jnp.int32, (rows, kpad), 1)
    keep = (c < rows) & ((c % n_b) == (r % n_b)) & ((c // n_b) <= (r // n_b))
    parts = []
    for g in range(n_groups):
        w = jnp.where(keep, wbig_ref[g], 0.0).astype(BF16)
        mixed = jnp.dot(w, vpad[:, g * gw:(g + 1) * gw], preferred_element_type=F32) + bst_ref[:, g:g + 1]
        parts.append(u_ref[:, g * gw:(g + 1) * gw] * mixed)
    yb = jnp.concatenate(parts, axis=1)
    ob_ref[...] = (_rms(yb, gb_ref[...]) * _silu(zb_ref[...])).astype(ob_ref.dtype)

    full = jnp.concatenate([sc_ref[...], x_ref[...]], axis=0)
    conv_ref[...] = full[rows:, :]
    ys = []
    for t in range(n_t):
        def slab(m):
            return full[m * n_b:(m + 1) * n_b, :]
        y = slab(t + conv_w - 1) * cw_ref[conv_w - 1:conv_w, :] + cb_ref[...]
        for j in range(conv_w - 1):
            y = y + slab(t + j) * cw_ref[j:j + 1, :]
        ys.append(y)
    y = jnp.concatenate(ys, axis=0)
    a, u = _lru_gates(y, wri_ref, br_ref, bi_ref, lam_ref, n_blocks, bs)
    h = h0_ref[...]
    hs = []
    for t in range(n_t):
        h = a[t * n_b:(t + 1) * n_b, :] * h + u[t * n_b:(t + 1) * n_b, :]
        hs.append(h)
    h_ref[...] = h
    yc = jnp.concatenate(hs, axis=0)
    oc_ref[...] = (_rms(yc, gc_ref[...]) * _silu(zc_ref[...])).astype(oc_ref.dtype)


def _sample_bc(proj, u_block, x_block, g_sgu, wbig, bst, g_b, sc, h0, conv_w, conv_b, wri, b_r, b_i, lam, g_c):
    rows = proj.shape[0]
    wb = g_sgu.shape[1]
    wc = conv_b.shape[1]
    n_b = h0.shape[0]
    n_groups = wbig.shape[0]
    n_blocks, bs, _ = wri.shape
    cw = conv_w.shape[0]

    def full(shape):
        return pl.BlockSpec(shape, lambda i: (0,) * len(shape))

    return pl.pallas_call(
        functools.partial(_sample_bc_kernel, n_groups=n_groups, gw=wb // n_groups, n_blocks=n_blocks,
                          bs=bs, conv_w=cw, n_b=n_b),
        out_shape=(jax.ShapeDtypeStruct((rows, wb), BF16),
                   jax.ShapeDtypeStruct((rows, wc), BF16),
                   jax.ShapeDtypeStruct((rows, wb), F32),
                   jax.ShapeDtypeStruct((n_b, wc), F32),
                   jax.ShapeDtypeStruct(sc.shape, F32)),
        grid=(1,),
        in_specs=[pl.BlockSpec((rows, wb), lambda i: (0, u_block)),
                  pl.BlockSpec((rows, wb), lambda i: (0, u_block + 1)),
                  pl.BlockSpec((rows, wb), lambda i: (0, u_block + 2)),
                  pl.BlockSpec((rows, wc), lambda i: (0, x_block)),
                  pl.BlockSpec((rows, wc), lambda i: (0, x_block + 1)),
                  full((1, wb)), full(wbig.shape), full(bst.shape), full((1, wb)),
                  full(sc.shape), full(h0.shape), full(conv_w.shape), full((1, wc)),
                  full(wri.shape), full((1, wc)), full((1, wc)), full((1, wc)), full((1, wc))],
        out_specs=(full((rows, wb)), full((rows, wc)), full((rows, wb)), full((n_b, wc)), full(sc.shape)),
        compiler_params=_cparams("arbitrary"),
        name="sample_bc",
    )(proj, proj, proj, proj, proj, g_sgu, wbig, bst, g_b, sc, h0, conv_w, conv_b, wri, b_r, b_i, lam, g_c)


def _cross_kernel(x_ref, g_ref, wq_ref, mk_ref, mv_ref, wo_ref, gf_ref, o_ref, *, n_heads, hd, n_b, final):
    x = x_ref[...]
    hn = _rms(x, g_ref[...]).astype(BF16)
    q = jnp.dot(hn, wq_ref[...].astype(BF16), preferred_element_type=F32)
    n_mem_rows = mk_ref.shape[0]
    rows = x.shape[0]
    if n_b:
        r = lax.broadcasted_iota(jnp.int32, (rows, n_mem_rows), 0)
        c = lax.broadcasted_iota(jnp.int32, (rows, n_mem_rows), 1)
        own = (c // (n_mem_rows // n_b)) == (r % n_b)
    outs = []
    for h in range(n_heads):
        qh = q[:, h * hd:(h + 1) * hd].astype(BF16)
        kh = mk_ref[:, h * hd:(h + 1) * hd].astype(BF16)
        vh = mv_ref[:, h * hd:(h + 1) * hd].astype(BF16)
        s = lax.dot_general(qh, kh, (((1,), (1,)), ((), ())), preferred_element_type=F32)
        s = s * (1.0 / math.sqrt(hd))
        if n_b:
            s = jnp.where(own, s, -jnp.inf)
        e = jnp.exp(s - jnp.max(s, axis=-1, keepdims=True))
        p = e / jnp.sum(e, axis=-1, keepdims=True)
        outs.append(jnp.dot(p.astype(BF16), vh, preferred_element_type=F32))
    o = jnp.concatenate(outs, axis=1).astype(BF16)
    y = x + jnp.dot(o, wo_ref[...].astype(BF16), preferred_element_type=F32)
    if final:
        y = _rms(y, gf_ref[...])
    o_ref[...] = y


def _cross(x, g, w_cq, mk, mv, w_co, layer, g_final, final, rows_per_batch, n_heads, hd):
    m, d = x.shape
    wx = n_heads * hd
    if rows_per_batch:
        tm = min(rows_per_batch, CROSS_ROWS)
        tiles_per_batch = rows_per_batch // tm
        n_mem = mk.shape[0] // (m // rows_per_batch)
        mem_spec = pl.BlockSpec((n_mem, wx), lambda i: (i // tiles_per_batch, 0))
        n_b = 0
    else:
        tm = m
        mem_spec = pl.BlockSpec(mk.shape, lambda i: (0, 0))
        n_b = SUBLANES
    return pl.pallas_call(
        functools.partial(_cross_kernel, n_heads=n_heads, hd=hd, n_b=n_b, final=final),
        out_shape=jax.ShapeDtypeStruct((m, d), F32),
        grid=(m // tm,),
        in_specs=[pl.BlockSpec((tm, d), lambda i: (i, 0)),
                  pl.BlockSpec((1, d), lambda i: (0, 0)),
                  pl.BlockSpec((None, d, wx), lambda i: (layer, 0, 0)),
                  mem_spec, mem_spec,
                  pl.BlockSpec((None, wx, d), lambda i: (layer, 0, 0)),
                  pl.BlockSpec((1, d), lambda i: (0, 0))],
        out_specs=pl.BlockSpec((tm, d), lambda i: (i, 0)),
        compiler_params=_cparams("parallel"),
        name="cross",
    )(x, g, w_cq, mk, mv, w_co, g_final)


def kernel(x_prompt, x_sample, mem_prompt, cache_k, cache_v, page_table, cache_mem_k, cache_mem_v, state_h, state_conv, g_mix, w_in, g_branch, w_out, g_sgu, w_s, b_s, conv_w, conv_b, w_r, b_r, w_i, b_i, lru_lambda, sb_bias, g_cross, w_cq, w_ck, w_cv, w_co, g_final):
    bp, seq, d = x_prompt.shape
    db, dt, _ = x_sample.shape
    depth, n_pool, page, n_heads, hd = cache_k.shape
    n_mem, hx, hdx = cache_mem_k.shape[2:]
    wa = n_heads * hd
    wb = g_sgu.shape[1]
    wc = state_h.shape[2]
    wx = hx * hdx
    n_groups, chunk = w_s.shape[1], w_s.shape[2]
    n_blocks, bs = w_r.shape[1], w_r.shape[2]
    cw = conv_w.shape[1]
    n_in = w_in.shape[2]
    assert n_in == 4 * wa + 3 * wb + 2 * wc and (4 * wa) % wb == 0 and (4 * wa + 3 * wb) % wc == 0
    assert db == SUBLANES and dt <= SUBLANES and dt <= chunk and cw - 1 <= SUBLANES
    u_block = 4 * wa // wb
    x_block = (4 * wa + 3 * wb) // wc
    z_block = 3

    cache_k2 = cache_k.reshape(depth, n_pool, page * n_heads, hd)
    cache_v2 = cache_v.reshape(depth, n_pool, page * n_heads, hd)
    mem_x = mem_prompt.reshape(bp * n_mem, d)
    w_ckv = jnp.concatenate([w_ck, w_cv], axis=2)
    w_cq = w_cq.astype(BF16)
    w_co = w_co.astype(BF16)
    wri = jnp.concatenate([w_r, w_i], axis=3).astype(BF16)
    g_fin = g_final.reshape(1, d)
    idx = jnp.arange(page)
    tri_page = (idx[:, None] >= idx[None, :]).astype(BF16)
    blk = min(seq, ATTN_BLK)
    idx = jnp.arange(blk)
    tri_blk = (idx[:, None] >= idx[None, :]).astype(BF16)

    xp = x_prompt.reshape(bp * seq, d)
    xs = jnp.swapaxes(x_sample, 0, 1).reshape(dt * db, d)

    outs = {k: [] for k in ("ks", "vs", "mk", "mv", "hp", "hs", "cp", "cs", "sgu")}
    projs = []
    for l in range(depth):
        gm = g_mix[l].reshape(1, d)
        g_a = g_branch[l, :wa].reshape(1, wa)
        g_b = g_branch[l, wa:wa + wb].reshape(1, wb)
        g_c = g_branch[l, wa + wb:].reshape(1, wc)
        gs = g_sgu[l].reshape(1, wb)
        cb = conv_b[l].reshape(1, wc)
        br = b_r[l].reshape(1, wc)
        bi = b_i[l].reshape(1, wc)
        lam = lru_lambda[l].reshape(1, wc)
        gc = g_cross[l].reshape(1, d)
        final = l == depth - 1
        w_out_l = [(w_out, l, 0), (w_out, l, wa // wb), (w_out, l, (wa + wb) // wc)]

        hn = _rmsnorm(xp, gm, BF16)
        proj = _matmul([hn], [(w_in, l, 0)], n_in, name="in_proj")
        proj3 = proj.reshape(bp, seq, n_in)
        ya = _sb_prompt(proj3, sb_bias[l], tri_blk, n_heads, hd)
        cat_a = _gate(ya.reshape(bp * seq, wa), proj, z_block, g_a)
        cat_b = _sgu_prompt(proj3, u_block, gs, w_s[l], jnp.transpose(b_s[l]), g_b)
        cat_c, h_p = _lru_prompt(proj3, x_block, conv_w[l], cb, wri[l], br, bi, lam, g_c)
        x1 = _matmul([cat_a, cat_b.reshape(bp * seq, wb), cat_c.reshape(bp * seq, wc)], w_out_l, d,
                     res=xp, name="out_proj")
        mkv = _matmul([mem_x], [(w_ckv, l, 0)], 2 * wx, rows=ROWS_NORM, name="mem_proj")
        mk, mv = mkv[:, :wx], mkv[:, wx:]
        xp = _cross(x1, gc, w_cq, mk, mv, w_co, l, g_fin, final, seq, hx, hdx)
        projs.append(proj3)
        outs["mk"].append(mk.reshape(bp, n_mem, hx, hdx))
        outs["mv"].append(mv.reshape(bp, n_mem, hx, hdx))
        outs["hp"].append(h_p[:, SUBLANES - 1, :])
        outs["cp"].append(proj3[:, seq - (cw - 1):, x_block * wc:(x_block + 1) * wc])

        hn_s = _rmsnorm(xs, gm, BF16)
        proj_s = _matmul([hn_s], [(w_in, l, 0)], n_in, name="in_proj_s")

        def batch_major(a):
            return jnp.swapaxes(a.reshape(dt, db, -1), 0, 1)

        q_s = batch_major(proj_s[:, :wa]).reshape(db, dt, n_heads, hd)
        k_s = batch_major(proj_s[:, wa:2 * wa])
        v_s = batch_major(proj_s[:, 2 * wa:3 * wa])
        pad_t = ((0, 0), (0, 0), (0, SUBLANES - dt), (0, 0))
        q_rows = jnp.pad(jnp.swapaxes(q_s, 1, 2), pad_t).reshape(db, n_heads * SUBLANES, hd)
        pad_j = ((0, 0), (0, (SUBLANES - dt) * n_heads), (0, 0))
        ya_s = _sb_decode(q_rows, jnp.pad(k_s.reshape(db, dt * n_heads, hd), pad_j),
                          jnp.pad(v_s.reshape(db, dt * n_heads, hd), pad_j), cache_k2, cache_v2, l, page_table,
                          sb_bias[l], tri_page, n_heads)
        ya_s = ya_s.reshape(db, n_heads, SUBLANES, hd)[:, :, :dt]
        ya_s = jnp.transpose(ya_s, (2, 0, 1, 3)).reshape(dt * db, wa)
        cat_a_s = _gate(ya_s, proj_s, z_block, g_a)
        wbig = jnp.repeat(jnp.repeat(w_s[l][:, :dt, :dt], db, axis=1), db, axis=2)
        wbig = jnp.pad(wbig, ((0, 0), (0, 0), (0, LANES - dt * db)))
        bst = jnp.repeat(jnp.transpose(b_s[l][:, :dt]), db, axis=0)
        sc = jnp.swapaxes(state_conv[l], 0, 1).reshape((cw - 1) * db, wc)
        cat_b_s, cat_c_s, vn_s, h_s, conv_s = _sample_bc(
            proj_s, u_block, x_block, gs, wbig, bst, g_b, sc, state_h[l], conv_w[l], cb, wri[l], br, bi, lam, g_c)
        x1_s = _matmul([cat_a_s, cat_b_s, cat_c_s], w_out_l, d, res=xs, name="out_proj_s")
        xs = _cross(x1_s, gc, w_cq, cache_mem_k[l].reshape(db * n_mem, wx), cache_mem_v[l].reshape(db * n_mem, wx),
                    w_co, l, g_fin, final, 0, hx, hdx)
        outs["ks"].append(k_s.reshape(db, dt, n_heads, hd))
        outs["vs"].append(v_s.reshape(db, dt, n_heads, hd))
        outs["hs"].append(h_s)
        outs["cs"].append(jnp.swapaxes(conv_s.reshape(cw - 1, db, wc), 0, 1))
        outs["sgu"].append(batch_major(vn_s))

    y_prompt = xp.reshape(bp, seq, d)
    y_sample = jnp.swapaxes(xs.reshape(dt, db, d), 0, 1)
    st = {k: jnp.stack(v) for k, v in outs.items()}
    k_p, v_p = _kv_out(projs, n_heads, hd)
    st["kp"] = k_p.reshape(depth, bp, seq, n_heads, hd)
    st["vp"] = v_p.reshape(depth, bp, seq, n_heads, hd)
    return (y_prompt, y_sample, st["kp"], st["vp"], st["ks"], st["vs"], st["mk"], st["mv"],
            st["hp"], st["hs"], st["cp"], st["cs"], st["sgu"])
```

```python
import functools
import math

import jax
import jax.numpy as jnp
from jax import lax
from jax.experimental import pallas as pl
from jax.experimental.pallas import tpu as pltpu

F32 = jnp.float32
BF16 = jnp.bfloat16

EPS = 1e-6
LRU_C = 8.0
SUBLANES = 8
LANES = 128
VMEM_LIMIT_BYTES = 56 * 1024 * 1024

ROWS_NORM = 256
ROWS_MM = 1024
COLS_MM = 512
ATTN_BLK = 256
ATTN_HEADS = 4
LRU_ROWS = 256
CROSS_ROWS = 256
PAGES_PER_STEP = 8


def _cparams(*sem):
    return pltpu.CompilerParams(dimension_semantics=sem, vmem_limit_bytes=VMEM_LIMIT_BYTES)


def _softplus(x):
    return jnp.maximum(x, 0.0) + jnp.log1p(jnp.exp(-jnp.abs(x)))


def _softplus_sum(x):
    return jnp.maximum(x, 0.0) + jnp.log(1.0 + jnp.exp(-jnp.abs(x)))


def _rms(x, g):
    return x * lax.rsqrt(jnp.mean(x * x, axis=-1, keepdims=True) + EPS) * g


def _silu(x):
    return x * jax.nn.sigmoid(x)


def _split_bf16(x):
    hi = x.astype(BF16)
    lo = (x - hi.astype(F32)).astype(BF16)
    return hi, lo


def _rmsnorm_kernel(x_ref, g_ref, o_ref):
    o_ref[...] = _rms(x_ref[...], g_ref[...]).astype(o_ref.dtype)


def _rmsnorm(x, g, out_dtype):
    m, d = x.shape
    tm = min(m, ROWS_NORM)
    return pl.pallas_call(
        _rmsnorm_kernel,
        out_shape=jax.ShapeDtypeStruct((m, d), out_dtype),
        grid=(m // tm,),
        in_specs=[pl.BlockSpec((tm, d), lambda i: (i, 0)),
                  pl.BlockSpec((1, d), lambda i: (0, 0))],
        out_specs=pl.BlockSpec((tm, d), lambda i: (i, 0)),
        compiler_params=_cparams("parallel"),
        name="rmsnorm",
    )(x, g.reshape(1, d))


def _matmul_kernel(*refs, n_in, has_res):
    o_ref = refs[-1]
    acc = None
    for a_ref, w_ref in zip(refs[:n_in], refs[n_in:2 * n_in]):
        d = jnp.dot(a_ref[...].astype(BF16), w_ref[...].astype(BF16), preferred_element_type=F32)
        acc = d if acc is None else acc + d
    if has_res:
        acc = acc + refs[2 * n_in][...]
    o_ref[...] = acc


def _matmul(a_list, w_list, n, res=None, rows=ROWS_MM, name="matmul"):
    m = a_list[0].shape[0]
    tm = min(m, rows)
    tn = min(n, COLS_MM)
    in_specs, args = [], []
    for a in a_list:
        in_specs.append(pl.BlockSpec((tm, a.shape[1]), lambda i, j: (i, 0)))
        args.append(a)
    for a, (w, layer, rb) in zip(a_list, w_list):
        in_specs.append(pl.BlockSpec((None, a.shape[1], tn),
                                     functools.partial(lambda i, j, layer, rb: (layer, rb, j), layer=layer, rb=rb)))
        args.append(w)
    if res is not None:
        in_specs.append(pl.BlockSpec((tm, tn), lambda i, j: (i, j)))
        args.append(res)
    return pl.pallas_call(
        functools.partial(_matmul_kernel, n_in=len(a_list), has_res=res is not None),
        out_shape=jax.ShapeDtypeStruct((m, n), F32),
        grid=(m // tm, n // tn),
        in_specs=in_specs,
        out_specs=pl.BlockSpec((tm, tn), lambda i, j: (i, j)),
        compiler_params=_cparams("parallel", "arbitrary"),
        name=name,
    )(*args)


def _matmul_pair_kernel(*refs, n_in, has_res):
    a_refs, s_refs, w_refs = refs[:n_in], refs[n_in:2 * n_in], refs[2 * n_in:3 * n_in]
    rest = refs[3 * n_in:]
    res_ref, res_s_ref = (rest[0], rest[1]) if has_res else (None, None)
    o_ref, os_ref = rest[2 * has_res], rest[2 * has_res + 1]
    wb_refs = rest[2 * has_res + 2:]

    def product(lhs_refs, res):
        acc = None
        for a_ref, wb_ref in zip(lhs_refs, wb_refs):
            d = jnp.dot(a_ref[...], wb_ref[...], preferred_element_type=F32)
            acc = d if acc is None else acc + d
        return acc if res is None else acc + res[...]

    @pl.when(pl.program_id(1) == 0)
    def _():
        for w_ref, wb_ref in zip(w_refs, wb_refs):
            wb_ref[...] = w_ref[...].astype(BF16)
        os_ref[...] = product(s_refs, res_s_ref)

    o_ref[...] = product(a_refs, res_ref)


def _matmul_pair(a_list, s_list, w_list, n, res=None, res_s=None, name="matmul_pair"):
    m, ms = a_list[0].shape[0], s_list[0].shape[0]
    tm = min(m, ROWS_MM)
    tn = min(n, COLS_MM)
    in_specs, args = [], []
    for a in a_list:
        in_specs.append(pl.BlockSpec((tm, a.shape[1]), lambda j, i: (i, 0)))
        args.append(a)
    for s in s_list:
        in_specs.append(pl.BlockSpec((ms, s.shape[1]), lambda j, i: (0, 0)))
        args.append(s)
    for a, (w, layer, rb) in zip(a_list, w_list):
        in_specs.append(pl.BlockSpec((None, a.shape[1], tn),
                                     functools.partial(lambda j, i, layer, rb: (layer, rb, j), layer=layer, rb=rb)))
        args.append(w)
    if res is not None:
        in_specs += [pl.BlockSpec((tm, tn), lambda j, i: (i, j)), pl.BlockSpec((ms, tn), lambda j, i: (0, j))]
        args += [res, res_s]
    return pl.pallas_call(
        functools.partial(_matmul_pair_kernel, n_in=len(a_list), has_res=res is not None),
        out_shape=(jax.ShapeDtypeStruct((m, n), F32), jax.ShapeDtypeStruct((ms, n), F32)),
        grid=(n // tn, m // tm),
        in_specs=in_specs,
        out_specs=(pl.BlockSpec((tm, tn), lambda j, i: (i, j)), pl.BlockSpec((ms, tn), lambda j, i: (0, j))),
        scratch_shapes=[pltpu.VMEM((a.shape[1], tn), BF16) for a in a_list],
        compiler_params=_cparams("parallel", "arbitrary"),
        name=name,
    )(*args)


MASKED_LOGIT = -1e30


def _sb_tail(z, tri):
    hi, lo = _split_bf16(_softplus_sum(z))
    return jnp.dot(hi, tri, preferred_element_type=F32) + jnp.dot(lo, tri, preferred_element_type=F32)


def _sb_prompt_kernel(bias_ref, q_ref, k_ref, v_ref, tri_ref, o_ref, kb_ref, vb_ref, z_ref, tail_ref,
                      *, blk, hd, scale):
    hg = pl.program_id(1)
    i = pl.program_id(2)
    n_h = q_ref.shape[1] // hd

    @pl.when(i == 0)
    def _():
        kb_ref[...] = k_ref[...].astype(BF16)
        vb_ref[...] = v_ref[...].astype(BF16)

    q = [q_ref[:, h * hd:(h + 1) * hd].astype(BF16) for h in range(n_h)]
    bias = [bias_ref[hg * n_h + h] for h in range(n_h)]
    tri = tri_ref[...]

    def rows(j):
        return pl.ds(pl.multiple_of(j * blk, blk), blk)

    def front(j, h, mask):
        kblk = kb_ref[rows(j), h * hd:(h + 1) * hd]
        z = lax.dot_general(q[h], kblk, (((1,), (1,)), ((), ())), preferred_element_type=F32) * scale + bias[h]
        if mask is not None:
            z = jnp.where(mask, z, MASKED_LOGIT)
        z_ref[h] = z
        tail_ref[h] = _sb_tail(z, tri)

    def back(j, h, acc, carry):
        tail = tail_ref[h]
        a = jnp.exp(z_ref[h] - tail - carry)
        pv = jnp.dot(a.astype(BF16), vb_ref[rows(j), h * hd:(h + 1) * hd], preferred_element_type=F32)
        return acc + pv, carry + tail[:, 0:1]

    row = lax.broadcasted_iota(jnp.int32, (blk, blk), 0)
    col = lax.broadcasted_iota(jnp.int32, (blk, blk), 1)
    for h in range(n_h):
        front(i, h, col < row)

    def body(t, state):
        j = i - 1 - t
        state = tuple(back(j + 1, h, *state[h]) for h in range(n_h))
        for h in range(n_h):
            front(j, h, None)
        return state

    zero = (jnp.zeros((blk, hd), F32), jnp.zeros((blk, 1), F32))
    state = lax.fori_loop(0, i, body, (zero,) * n_h)
    for h in range(n_h):
        acc, _ = back(0, h, *state[h])
        o_ref[:, h * hd:(h + 1) * hd] = acc


def _sb_prompt(proj, sb_bias, tri, n_heads, hd):
    b, t, _ = proj.shape
    blk = min(t, ATTN_BLK)
    n_h = ATTN_HEADS
    groups = n_heads // n_h
    w = n_h * hd
    return pl.pallas_call(
        functools.partial(_sb_prompt_kernel, blk=blk, hd=hd, scale=1.0 / math.sqrt(hd)),
        out_shape=jax.ShapeDtypeStruct((b, t, n_heads * hd), F32),
        grid=(b, groups, t // blk),
        in_specs=[pl.BlockSpec(memory_space=pltpu.SMEM),
                  pl.BlockSpec((None, blk, w), lambda bi, hg, i: (bi, i, hg)),
                  pl.BlockSpec((None, t, w), lambda bi, hg, i: (bi, 0, groups + hg)),
                  pl.BlockSpec((None, t, w), lambda bi, hg, i: (bi, 0, 2 * groups + hg)),
                  pl.BlockSpec((blk, blk), lambda bi, hg, i: (0, 0))],
        out_specs=pl.BlockSpec((None, blk, w), lambda bi, hg, i: (bi, i, hg)),
        scratch_shapes=[pltpu.VMEM((t, w), BF16), pltpu.VMEM((t, w), BF16),
                        pltpu.VMEM((n_h, blk, blk), F32), pltpu.VMEM((n_h, blk, blk), F32)],
        compiler_params=_cparams("parallel", "parallel", "arbitrary"),
        name="sb_prompt",
    )(sb_bias, proj, proj, proj, tri)


def _sb_decode_kernel(pt_ref, q_ref, kn_ref, vn_ref, bias_ref, tri_ref, *rest,
                      n_pages_step, n_heads, scale):
    n_tiles = n_heads // SUBLANES
    n_blocks = n_pages_step * n_tiles
    k_refs = rest[:n_blocks]
    v_refs = rest[n_blocks:2 * n_blocks]
    o_ref = rest[2 * n_blocks]
    carry_ref, acc_ref = rest[2 * n_blocks + 1:]
    p = pl.program_id(1)
    page = tri_ref.shape[0]
    rows, hd = q_ref.shape
    slab = rows // n_heads
    tri_t = tri_ref[...]
    bias = bias_ref[...]
    q = q_ref[...].astype(BF16)
    key = lax.broadcasted_iota(jnp.int32, (page, rows), 0)
    qrow = lax.broadcasted_iota(jnp.int32, (page, rows), 1)
    qhead = qrow // slab

    def front(k_head, mask):
        s = None
        for h0 in range(0, n_heads, SUBLANES):
            k_all = jnp.concatenate([k_head(h0 + h).astype(BF16) for h in range(SUBLANES)], axis=0)
            s_all = lax.dot_general(k_all, q, (((1,), (1,)), ((), ())), preferred_element_type=F32)
            for h in range(SUBLANES):
                part = s_all[h * page:(h + 1) * page]
                s = part if s is None else jnp.where(qhead == h0 + h, part, s)
        z = s * scale + bias
        if mask is not None:
            z = jnp.where(mask, z, MASKED_LOGIT)
        hi, lo = _split_bf16(_softplus_sum(z))
        tail = jnp.dot(tri_t, hi, preferred_element_type=F32) + jnp.dot(tri_t, lo, preferred_element_type=F32)
        return z, tail

    def back(z, tail, carry, v_head):
        a = jnp.transpose(jnp.exp(z - tail - carry))
        pv = jnp.concatenate(
            [jnp.dot(a[h * slab:(h + 1) * slab].astype(BF16), v_head(h).astype(BF16), preferred_element_type=F32)
             for h in range(n_heads)], axis=0)
        return pv, carry + tail[0:1, :]

    @pl.when(p == 0)
    def _():
        n_new = kn_ref.shape[0] // n_heads
        pad = jnp.zeros((page - n_new, hd), F32)
        z, tail = front(lambda h: jnp.concatenate([kn_ref[pl.ds(h, n_new, stride=n_heads), :], pad], axis=0),
                        key < (qrow % slab))
        pv, carry = back(z, tail, jnp.zeros((1, rows), F32),
                         lambda h: jnp.concatenate([vn_ref[pl.ds(h, n_new, stride=n_heads), :], pad], axis=0))
        acc_ref[...] = pv
        carry_ref[...] = carry

    def head_of(refs, i, h):
        flat = refs[i * n_tiles + h // SUBLANES].reshape(page * SUBLANES, hd)
        return flat[pl.ds(h % SUBLANES, page, stride=SUBLANES), :]

    fronts = [front(lambda h: head_of(k_refs, i, h), None) for i in range(n_pages_step)]
    carry = carry_ref[...]
    acc = acc_ref[...]
    for i, (z, tail) in enumerate(fronts):
        pv, carry = back(z, tail, carry, lambda h: head_of(v_refs, i, h))
        acc = acc + pv
    carry_ref[...] = carry
    acc_ref[...] = acc

    @pl.when(p == pl.num_programs(1) - 1)
    def _():
        o_ref[...] = acc_ref[...]


def _sb_decode(q_rows, k_new, v_new, cache_k, cache_v, layer, page_table, sb_bias, tri, n_heads):
    b, rows, hd = q_rows.shape
    n_pages = page_table.shape[1]
    page = tri.shape[0]
    pps = PAGES_PER_STEP
    n_tiles = n_heads // SUBLANES
    bias_rows = jnp.repeat(sb_bias, rows // n_heads)[None, :]
    tri = jnp.transpose(tri)

    def page_spec(i):
        return pl.BlockSpec((None, None, page, None, SUBLANES, hd),
                            lambda bi, p, pt: (layer, pt[bi, n_pages - 1 - (p * pps + i // n_tiles)], 0,
                                               i % n_tiles, 0, 0))

    grid_spec = pltpu.PrefetchScalarGridSpec(
        num_scalar_prefetch=1,
        grid=(b, n_pages // pps),
        in_specs=[pl.BlockSpec((None, rows, hd), lambda bi, p, pt: (bi, 0, 0)),
                  pl.BlockSpec((None, k_new.shape[1], hd), lambda bi, p, pt: (bi, 0, 0)),
                  pl.BlockSpec((None, v_new.shape[1], hd), lambda bi, p, pt: (bi, 0, 0)),
                  pl.BlockSpec((1, rows), lambda bi, p, pt: (0, 0)),
                  pl.BlockSpec((page, page), lambda bi, p, pt: (0, 0))]
                 + [page_spec(i) for i in range(pps * n_tiles)] * 2,
        out_specs=pl.BlockSpec((None, rows, hd), lambda bi, p, pt: (bi, 0, 0)),
        scratch_shapes=[pltpu.VMEM((1, rows), F32),
                        pltpu.VMEM((rows, hd), F32)],
    )
    return pl.pallas_call(
        functools.partial(_sb_decode_kernel, n_pages_step=pps, n_heads=n_heads, scale=1.0 / math.sqrt(hd)),
        out_shape=jax.ShapeDtypeStruct((b, rows, hd), F32),
        grid_spec=grid_spec,
        compiler_params=_cparams("parallel", "arbitrary"),
        name="sb_decode",
    )(page_table, q_rows, k_new, v_new, bias_rows, tri,
      *([cache_k] * (pps * n_tiles)), *([cache_v] * (pps * n_tiles)))


def _kv_out_kernel(*refs, n_layers, n_heads, hd):
    ko_ref, vo_ref = refs[2 * n_layers:]
    layer = pl.program_id(0)
    for l in range(n_layers):
        @pl.when(layer == l)
        def _():
            k_ref, v_ref = refs[2 * l], refs[2 * l + 1]
            rows = k_ref.shape[0]
            for h in range(n_heads):
                ko_ref[pl.ds(h, rows, stride=n_heads), :] = k_ref[:, h * hd:(h + 1) * hd]
                vo_ref[pl.ds(h, rows, stride=n_heads), :] = v_ref[:, h * hd:(h + 1) * hd]


def _kv_out(projs, n_heads, hd):
    n_layers = len(projs)
    b, t, _ = projs[0].shape
    wa = n_heads * hd
    tile = min(t, ROWS_NORM)
    n_tiles = t // tile
    out = jax.ShapeDtypeStruct((n_layers, b, t * n_heads, hd), F32)
    out_spec = pl.BlockSpec((None, None, tile * n_heads, hd), lambda l, bi, i: (l, bi, i, 0))

    def in_spec(layer, col):
        def index(l, bi, i):
            before, own = l < layer, l == layer
            return (jnp.where(own, bi, jnp.where(before, 0, b - 1)),
                    jnp.where(own, i, jnp.where(before, 0, n_tiles - 1)), col)
        return pl.BlockSpec((None, tile, wa), index)

    in_specs, args = [], []
    for l, proj in enumerate(projs):
        in_specs += [in_spec(l, 1), in_spec(l, 2)]
        args += [proj, proj]
    return pl.pallas_call(
        functools.partial(_kv_out_kernel, n_layers=n_layers, n_heads=n_heads, hd=hd),
        out_shape=(out, out),
        grid=(n_layers, b, n_tiles),
        in_specs=in_specs,
        out_specs=(out_spec, out_spec),
        compiler_params=_cparams("arbitrary", "arbitrary", "arbitrary"),
        name="kv_out",
    )(*args)


def _gate_kernel(y_ref, z_ref, g_ref, o_ref):
    o_ref[...] = (_rms(y_ref[...], g_ref[...]) * _silu(z_ref[...])).astype(o_ref.dtype)


def _gate(y, proj, z_block, g):
    m, w = y.shape
    tm = min(m, ROWS_NORM)
    return pl.pallas_call(
        _gate_kernel,
        out_shape=jax.ShapeDtypeStruct((m, w), BF16),
        grid=(m // tm,),
        in_specs=[pl.BlockSpec((tm, w), lambda i: (i, 0)),
                  pl.BlockSpec((tm, w), lambda i: (i, z_block)),
                  pl.BlockSpec((1, w), lambda i: (0, 0))],
        out_specs=pl.BlockSpec((tm, w), lambda i: (i, 0)),
        compiler_params=_cparams("parallel"),
        name="gate_a",
    )(y, proj, g)


def _sgu_prompt_kernel(u_ref, v_ref, z_ref, gs_ref, ws_ref, bst_ref, gb_ref, o_ref, *, n_groups, gw):
    chunk = v_ref.shape[0]
    vn = _rms(v_ref[...], gs_ref[...]).astype(BF16)
    row = lax.broadcasted_iota(jnp.int32, (chunk, chunk), 0)
    col = lax.broadcasted_iota(jnp.int32, (chunk, chunk), 1)
    parts = []
    for g in range(n_groups):
        w = jnp.where(col <= row, ws_ref[g], 0.0).astype(BF16)
        mixed = jnp.dot(w, vn[:, g * gw:(g + 1) * gw], preferred_element_type=F32) + bst_ref[:, g:g + 1]
        parts.append(u_ref[:, g * gw:(g + 1) * gw] * mixed)
    yb = jnp.concatenate(parts, axis=1)
    o_ref[...] = (_rms(yb, gb_ref[...]) * _silu(z_ref[...])).astype(o_ref.dtype)


def _sgu_prompt(proj, u_block, g_sgu, w_s, b_s_t, g_b):
    b, t, _ = proj.shape
    n_groups, chunk, _ = w_s.shape
    wb = g_sgu.shape[1]
    assert t % chunk == 0

    def col_spec(blk):
        return pl.BlockSpec((None, chunk, wb), lambda bi, c: (bi, c, blk))

    return pl.pallas_call(
        functools.partial(_sgu_prompt_kernel, n_groups=n_groups, gw=wb // n_groups),
        out_shape=jax.ShapeDtypeStruct((b, t, wb), BF16),
        grid=(b, t // chunk),
        in_specs=[col_spec(u_block), col_spec(u_block + 1), col_spec(u_block + 2),
                  pl.BlockSpec((1, wb), lambda bi, c: (0, 0)),
                  pl.BlockSpec((n_groups, chunk, chunk), lambda bi, c: (0, 0, 0)),
                  pl.BlockSpec((chunk, n_groups), lambda bi, c: (0, 0)),
                  pl.BlockSpec((1, wb), lambda bi, c: (0, 0))],
        out_specs=pl.BlockSpec((None, chunk, wb), lambda bi, c: (bi, c, 0)),
        compiler_params=_cparams("parallel", "parallel"),
        name="sgu_prompt",
    )(proj, proj, proj, g_sgu, w_s, b_s_t, g_b)


def _lru_gates(y, wri_ref, br_ref, bi_ref, lam_ref, n_blocks, bs):
    y16 = y.astype(BF16)
    r_parts, i_parts = [], []
    for n in range(n_blocks):
        ri = jnp.dot(y16[:, n * bs:(n + 1) * bs], wri_ref[n], preferred_element_type=F32)
        r_parts.append(ri[:, :bs])
        i_parts.append(ri[:, bs:])
    r = jax.nn.sigmoid(jnp.concatenate(r_parts, axis=1) + br_ref[...])
    i = jax.nn.sigmoid(jnp.concatenate(i_parts, axis=1) + bi_ref[...])
    log_a = -LRU_C * r * _softplus(-lam_ref[...])
    a = jnp.exp(log_a)
    u = jnp.sqrt(-jnp.tanh(log_a) * (a * a + 1.0)) * (i * y)
    return a, u


def _lru_prompt_kernel(x_ref, z_ref, cw_ref, cb_ref, wri_ref, br_ref, bi_ref, lam_ref, gb_ref,
                       o_ref, h_ref, xp_ref, hc_ref, *, n_blocks, bs, conv_w):
    c = pl.program_id(1)
    tt = x_ref.shape[0]

    @pl.when(c == 0)
    def _():
        xp_ref[0:SUBLANES, :] = jnp.zeros((SUBLANES, xp_ref.shape[1]), F32)
        hc_ref[...] = jnp.zeros_like(hc_ref)

    @pl.when(c > 0)
    def _():
        xp_ref[0:SUBLANES, :] = xp_ref[tt:tt + SUBLANES, :]

    xp_ref[SUBLANES:, :] = x_ref[...]
    y = xp_ref[SUBLANES:, :] * cw_ref[conv_w - 1:conv_w, :] + cb_ref[...]
    for j in range(conv_w - 1):
        shift = conv_w - 1 - j
        y = y + xp_ref[pl.ds(SUBLANES - shift, tt), :] * cw_ref[j:j + 1, :]

    a, u = _lru_gates(y, wri_ref, br_ref, bi_ref, lam_ref, n_blocks, bs)

    row = lax.broadcasted_iota(jnp.int32, a.shape, 0)
    k = 1
    while k < tt:
        a_prev = jnp.where(row >= k, pltpu.roll(a, k, 0), 1.0)
        u_prev = jnp.where(row >= k, pltpu.roll(u, k, 0), 0.0)
        u = a * u_prev + u
        a = a * a_prev
        k *= 2
    h = a * hc_ref[SUBLANES - 1:SUBLANES, :] + u
    hc_ref[...] = h[tt - SUBLANES:, :]
    h_ref[...] = h[tt - SUBLANES:, :]
    o_ref[...] = (_rms(h, gb_ref[...]) * _silu(z_ref[...])).astype(o_ref.dtype)


def _lru_prompt(proj, x_block, conv_w, conv_b, wri, b_r, b_i, lam, g_c):
    b, t, _ = proj.shape
    wc = conv_b.shape[1]
    n_blocks, bs, _ = wri.shape
    tt = min(t, LRU_ROWS)
    cw = conv_w.shape[0]
    vec = pl.BlockSpec((1, wc), lambda bi, c: (0, 0))
    return pl.pallas_call(
        functools.partial(_lru_prompt_kernel, n_blocks=n_blocks, bs=bs, conv_w=cw),
        out_shape=(jax.ShapeDtypeStruct((b, t, wc), BF16),
                   jax.ShapeDtypeStruct((b, SUBLANES, wc), F32)),
        grid=(b, t // tt),
        in_specs=[pl.BlockSpec((None, tt, wc), lambda bi, c: (bi, c, x_block)),
                  pl.BlockSpec((None, tt, wc), lambda bi, c: (bi, c, x_block + 1)),
                  pl.BlockSpec((cw, wc), lambda bi, c: (0, 0)),
                  vec,
                  pl.BlockSpec((n_blocks, bs, 2 * bs), lambda bi, c: (0, 0, 0)),
                  vec, vec, vec, vec],
        out_specs=(pl.BlockSpec((None, tt, wc), lambda bi, c: (bi, c, 0)),
                   pl.BlockSpec((None, SUBLANES, wc), lambda bi, c: (bi, 0, 0))),
        scratch_shapes=[pltpu.VMEM((tt + SUBLANES, wc), F32), pltpu.VMEM((SUBLANES, wc), F32)],
        compiler_params=_cparams("parallel", "arbitrary"),
        name="lru_prompt",
    )(proj, proj, conv_w, conv_b, wri, b_r, b_i, lam, g_c)


def _sample_bc_kernel(u_ref, v_ref, zb_ref, x_ref, zc_ref, gs_ref, wbig_ref, bst_ref, gb_ref,
                      sc_ref, h0_ref, cw_ref, cb_ref, wri_ref, br_ref, bi_ref, lam_ref, gc_ref,
                      ob_ref, oc_ref, vn_ref, h_ref, conv_ref, *, n_groups, gw, n_blocks, bs, conv_w, n_b):
    rows = v_ref.shape[0]
    n_t = rows // n_b
    kpad = wbig_ref.shape[2]
    vn = _rms(v_ref[...], gs_ref[...])
    vn_ref[...] = vn
    vpad = jnp.concatenate([vn, jnp.zeros((kpad - rows, vn.shape[1]), F32)], axis=0).astype(BF16)
    r = lax.broadcasted_iota(jnp.int32, (rows, kpad), 0)
    c = lax.broadcasted_iota(jnp.int32, (rows, kpad), 1)
    keep = (c < rows) & ((c % n_b) == (r % n_b)) & ((c // n_b) <= (r // n_b))
    parts = []
    for g in range(n_groups):
        w = jnp.where(keep, wbig_ref[g], 0.0).astype(BF16)
        mixed = jnp.dot(w, vpad[:, g * gw:(g + 1) * gw], preferred_element_type=F32) + bst_ref[:, g:g + 1]
        parts.append(u_ref[:, g * gw:(g + 1) * gw] * mixed)
    yb = jnp.concatenate(parts, axis=1)
    ob_ref[...] = (_rms(yb, gb_ref[...]) * _silu(zb_ref[...])).astype(ob_ref.dtype)

    full = jnp.concatenate([sc_ref[...], x_ref[...]], axis=0)
    conv_ref[...] = full[rows:, :]
    ys = []
    for t in range(n_t):
        def slab(m):
            return full[m * n_b:(m + 1) * n_b, :]
        y = slab(t + conv_w - 1) * cw_ref[conv_w - 1:conv_w, :] + cb_ref[...]
        for j in range(conv_w - 1):
            y = y + slab(t + j) * cw_ref[j:j + 1, :]
        ys.append(y)
    y = jnp.concatenate(ys, axis=0)
    a, u = _lru_gates(y, wri_ref, br_ref, bi_ref, lam_ref, n_blocks, bs)
    h = h0_ref[...]
    hs = []
    for t in range(n_t):
        h = a[t * n_b:(t + 1) * n_b, :] * h + u[t * n_b:(t + 1) * n_b, :]
        hs.append(h)
    h_ref[...] = h
    yc = jnp.concatenate(hs, axis=0)
    oc_ref[...] = (_rms(yc, gc_ref[...]) * _silu(zc_ref[...])).astype(oc_ref.dtype)


def _sample_bc(proj, u_block, x_block, g_sgu, wbig, bst, g_b, sc, h0, conv_w, conv_b, wri, b_r, b_i, lam, g_c):
    rows = proj.shape[0]
    wb = g_sgu.shape[1]
    wc = conv_b.shape[1]
    n_b = h0.shape[0]
    n_groups = wbig.shape[0]
    n_blocks, bs, _ = wri.shape
    cw = conv_w.shape[0]

    def full(shape):
        return pl.BlockSpec(shape, lambda i: (0,) * len(shape))

    return pl.pallas_call(
        functools.partial(_sample_bc_kernel, n_groups=n_groups, gw=wb // n_groups, n_blocks=n_blocks,
                          bs=bs, conv_w=cw, n_b=n_b),
        out_shape=(jax.ShapeDtypeStruct((rows, wb), BF16),
                   jax.ShapeDtypeStruct((rows, wc), BF16),
                   jax.ShapeDtypeStruct((rows, wb), F32),
                   jax.ShapeDtypeStruct((n_b, wc), F32),
                   jax.ShapeDtypeStruct(sc.shape, F32)),
        grid=(1,),
        in_specs=[pl.BlockSpec((rows, wb), lambda i: (0, u_block)),
                  pl.BlockSpec((rows, wb), lambda i: (0, u_block + 1)),
                  pl.BlockSpec((rows, wb), lambda i: (0, u_block + 2)),
                  pl.BlockSpec((rows, wc), lambda i: (0, x_block)),
                  pl.BlockSpec((rows, wc), lambda i: (0, x_block + 1)),
                  full((1, wb)), full(wbig.shape), full(bst.shape), full((1, wb)),
                  full(sc.shape), full(h0.shape), full(conv_w.shape), full((1, wc)),
                  full(wri.shape), full((1, wc)), full((1, wc)), full((1, wc)), full((1, wc))],
        out_specs=(full((rows, wb)), full((rows, wc)), full((rows, wb)), full((n_b, wc)), full(sc.shape)),
        compiler_params=_cparams("arbitrary"),
        name="sample_bc",
    )(proj, proj, proj, proj, proj, g_sgu, wbig, bst, g_b, sc, h0, conv_w, conv_b, wri, b_r, b_i, lam, g_c)


def _cross_kernel(x_ref, g_ref, wq_ref, mk_ref, mv_ref, wo_ref, gf_ref, o_ref, *, n_heads, hd, n_b, final):
    x = x_ref[...]
    hn = _rms(x, g_ref[...]).astype(BF16)
    q = jnp.dot(hn, wq_ref[...].astype(BF16), preferred_element_type=F32)
    n_mem_rows = mk_ref.shape[0]
    rows = x.shape[0]
    if n_b:
        r = lax.broadcasted_iota(jnp.int32, (rows, n_mem_rows), 0)
        c = lax.broadcasted_iota(jnp.int32, (rows, n_mem_rows), 1)
        own = (c // (n_mem_rows // n_b)) == (r % n_b)
    outs = []
    for h in range(n_heads):
        qh = q[:, h * hd:(h + 1) * hd].astype(BF16)
        kh = mk_ref[:, h * hd:(h + 1) * hd].astype(BF16)
        vh = mv_ref[:, h * hd:(h + 1) * hd].astype(BF16)
        s = lax.dot_general(qh, kh, (((1,), (1,)), ((), ())), preferred_element_type=F32)
        s = s * (1.0 / math.sqrt(hd))
        if n_b:
            s = jnp.where(own, s, -jnp.inf)
        e = jnp.exp(s - jnp.max(s, axis=-1, keepdims=True))
        p = e / jnp.sum(e, axis=-1, keepdims=True)
        outs.append(jnp.dot(p.astype(BF16), vh, preferred_element_type=F32))
    o = jnp.concatenate(outs, axis=1).astype(BF16)
    y = x + jnp.dot(o, wo_ref[...].astype(BF16), preferred_element_type=F32)
    if final:
        y = _rms(y, gf_ref[...])
    o_ref[...] = y


def _cross(x, g, w_cq, mk, mv, w_co, layer, g_final, final, rows_per_batch, n_heads, hd):
    m, d = x.shape
    wx = n_heads * hd
    if rows_per_batch:
        tm = min(rows_per_batch, CROSS_ROWS)
        tiles_per_batch = rows_per_batch // tm
        n_mem = mk.shape[0] // (m // rows_per_batch)
        mem_spec = pl.BlockSpec((n_mem, wx), lambda i: (i // tiles_per_batch, 0))
        n_b = 0
    else:
        tm = m
        mem_spec = pl.BlockSpec(mk.shape, lambda i: (0, 0))
        n_b = SUBLANES
    return pl.pallas_call(
        functools.partial(_cross_kernel, n_heads=n_heads, hd=hd, n_b=n_b, final=final),
        out_shape=jax.ShapeDtypeStruct((m, d), F32),
        grid=(m // tm,),
        in_specs=[pl.BlockSpec((tm, d), lambda i: (i, 0)),
                  pl.BlockSpec((1, d), lambda i: (0, 0)),
                  pl.BlockSpec((None, d, wx), lambda i: (layer, 0, 0)),
                  mem_spec, mem_spec,
                  pl.BlockSpec((None, wx, d), lambda i: (layer, 0, 0)),
                  pl.BlockSpec((1, d), lambda i: (0, 0))],
        out_specs=pl.BlockSpec((tm, d), lambda i: (i, 0)),
        compiler_params=_cparams("parallel"),
        name="cross",
    )(x, g, w_cq, mk, mv, w_co, g_final)


def kernel(x_prompt, x_sample, mem_prompt, cache_k, cache_v, page_table, cache_mem_k, cache_mem_v, state_h, state_conv, g_mix, w_in, g_branch, w_out, g_sgu, w_s, b_s, conv_w, conv_b, w_r, b_r, w_i, b_i, lru_lambda, sb_bias, g_cross, w_cq, w_ck, w_cv, w_co, g_final):
    bp, seq, d = x_prompt.shape
    db, dt, _ = x_sample.shape
    depth, n_pool, page, n_heads, hd = cache_k.shape
    n_mem, hx, hdx = cache_mem_k.shape[2:]
    wa = n_heads * hd
    wb = g_sgu.shape[1]
    wc = state_h.shape[2]
    wx = hx * hdx
    n_groups, chunk = w_s.shape[1], w_s.shape[2]
    n_blocks, bs = w_r.shape[1], w_r.shape[2]
    cw = conv_w.shape[1]
    n_in = w_in.shape[2]
    assert n_in == 4 * wa + 3 * wb + 2 * wc and (4 * wa) % wb == 0 and (4 * wa + 3 * wb) % wc == 0
    assert db == SUBLANES and dt <= SUBLANES and dt <= chunk and cw - 1 <= SUBLANES
    u_block = 4 * wa // wb
    x_block = (4 * wa + 3 * wb) // wc
    z_block = 3

    assert n_heads % SUBLANES == 0
    cache_k2 = cache_k.reshape(depth, n_pool, page, n_heads // SUBLANES, SUBLANES, hd)
    cache_v2 = cache_v.reshape(depth, n_pool, page, n_heads // SUBLANES, SUBLANES, hd)
    mem_x = mem_prompt.reshape(bp * n_mem, d)
    w_ckv = jnp.concatenate([w_ck, w_cv], axis=2)
    w_cq = w_cq.astype(BF16)
    w_co = w_co.astype(BF16)
    wri = jnp.concatenate([w_r, w_i], axis=3).astype(BF16)
    g_fin = g_final.reshape(1, d)
    idx = jnp.arange(page)
    tri_page = (idx[:, None] >= idx[None, :]).astype(BF16)
    blk = min(seq, ATTN_BLK)
    idx = jnp.arange(blk)
    tri_blk = (idx[:, None] >= idx[None, :]).astype(BF16)

    xp = x_prompt.reshape(bp * seq, d)
    xs = jnp.swapaxes(x_sample, 0, 1).reshape(dt * db, d)

    outs = {k: [] for k in ("ks", "vs", "mk", "mv", "hp", "hs", "cp", "cs", "sgu")}
    projs = []
    for l in range(depth):
        gm = g_mix[l].reshape(1, d)
        g_a = g_branch[l, :wa].reshape(1, wa)
        g_b = g_branch[l, wa:wa + wb].reshape(1, wb)
        g_c = g_branch[l, wa + wb:].reshape(1, wc)
        gs = g_sgu[l].reshape(1, wb)
        cb = conv_b[l].reshape(1, wc)
        br = b_r[l].reshape(1, wc)
        bi = b_i[l].reshape(1, wc)
        lam = lru_lambda[l].reshape(1, wc)
        gc = g_cross[l].reshape(1, d)
        final = l == depth - 1
        w_out_l = [(w_out, l, 0), (w_out, l, wa // wb), (w_out, l, (wa + wb) // wc)]

        hn = _rmsnorm(xp, gm, BF16)
        hn_s = _rmsnorm(xs, gm, BF16)
        proj, proj_s = _matmul_pair([hn], [hn_s], [(w_in, l, 0)], n_in, name="in_proj")
        proj3 = proj.reshape(bp, seq, n_in)

        ya = _sb_prompt(proj3, sb_bias[l], tri_blk, n_heads, hd)
        cat_a = _gate(ya.reshape(bp * seq, wa), proj, z_block, g_a)
        cat_b = _sgu_prompt(proj3, u_block, gs, w_s[l], jnp.transpose(b_s[l]), g_b)
        cat_c, h_p = _lru_prompt(proj3, x_block, conv_w[l], cb, wri[l], br, bi, lam, g_c)
        projs.append(proj3)
        outs["hp"].append(h_p[:, SUBLANES - 1, :])
        outs["cp"].append(proj3[:, seq - (cw - 1):, x_block * wc:(x_block + 1) * wc])

        def batch_major(a):
            return jnp.swapaxes(a.reshape(dt, db, -1), 0, 1)

        q_s = batch_major(proj_s[:, :wa]).reshape(db, dt, n_heads, hd)
        k_s = batch_major(proj_s[:, wa:2 * wa])
        v_s = batch_major(proj_s[:, 2 * wa:3 * wa])
        pad_t = ((0, 0), (0, 0), (0, SUBLANES - dt), (0, 0))
        q_rows = jnp.pad(jnp.swapaxes(q_s, 1, 2), pad_t).reshape(db, n_heads * SUBLANES, hd)
        pad_j = ((0, 0), (0, (SUBLANES - dt) * n_heads), (0, 0))
        ya_s = _sb_decode(q_rows, jnp.pad(k_s.reshape(db, dt * n_heads, hd), pad_j),
                          jnp.pad(v_s.reshape(db, dt * n_heads, hd), pad_j), cache_k2, cache_v2, l, page_table,
                          sb_bias[l], tri_page, n_heads)
        ya_s = ya_s.reshape(db, n_heads, SUBLANES, hd)[:, :, :dt]
        ya_s = jnp.transpose(ya_s, (2, 0, 1, 3)).reshape(dt * db, wa)
        cat_a_s = _gate(ya_s, proj_s, z_block, g_a)
        wbig = jnp.repeat(jnp.repeat(w_s[l][:, :dt, :dt], db, axis=1), db, axis=2)
        wbig = jnp.pad(wbig, ((0, 0), (0, 0), (0, LANES - dt * db)))
        bst = jnp.repeat(jnp.transpose(b_s[l][:, :dt]), db, axis=0)
        sc = jnp.swapaxes(state_conv[l], 0, 1).reshape((cw - 1) * db, wc)
        cat_b_s, cat_c_s, vn_s, h_s, conv_s = _sample_bc(
            proj_s, u_block, x_block, gs, wbig, bst, g_b, sc, state_h[l], conv_w[l], cb, wri[l], br, bi, lam, g_c)

        x1, x1_s = _matmul_pair([cat_a, cat_b.reshape(bp * seq, wb), cat_c.reshape(bp * seq, wc)],
                                [cat_a_s, cat_b_s, cat_c_s], w_out_l, d, res=xp, res_s=xs, name="out_proj")
        mkv = _matmul([mem_x], [(w_ckv, l, 0)], 2 * wx, rows=ROWS_NORM, name="mem_proj")
        mk, mv = mkv[:, :wx], mkv[:, wx:]
        xp = _cross(x1, gc, w_cq, mk, mv, w_co, l, g_fin, final, seq, hx, hdx)
        xs = _cross(x1_s, gc, w_cq, cache_mem_k[l].reshape(db * n_mem, wx), cache_mem_v[l].reshape(db * n_mem, wx),
                    w_co, l, g_fin, final, 0, hx, hdx)
        outs["mk"].append(mk.reshape(bp, n_mem, hx, hdx))
        outs["mv"].append(mv.reshape(bp, n_mem, hx, hdx))
        outs["ks"].append(k_s.reshape(db, dt, n_heads, hd))
        outs["vs"].append(v_s.reshape(db, dt, n_heads, hd))
        outs["hs"].append(h_s)
        outs["cs"].append(jnp.swapaxes(conv_s.reshape(cw - 1, db, wc), 0, 1))
        outs["sgu"].append(batch_major(vn_s))

    y_prompt = xp.reshape(bp, seq, d)
    y_sample = jnp.swapaxes(xs.reshape(dt, db, d), 0, 1)
    st = {k: jnp.stack(v) for k, v in outs.items()}
    k_p, v_p = _kv_out(projs, n_heads, hd)
    st["kp"] = k_p.reshape(depth, bp, seq, n_heads, hd)
    st["vp"] = v_p.reshape(depth, bp, seq, n_heads, hd)
    return (y_prompt, y_sample, st["kp"], st["vp"], st["ks"], st["vs"], st["mk"], st["mv"],
            st["hp"], st["hs"], st["cp"], st["cs"], st["sgu"])
```

```python
import functools
import math

import jax
import jax.numpy as jnp
from jax import lax
from jax.experimental import pallas as pl
from jax.experimental.pallas import tpu as pltpu

F32 = jnp.float32
BF16 = jnp.bfloat16

EPS = 1e-6
LRU_C = 8.0
SUBLANES = 8
LANES = 128
VMEM_LIMIT_BYTES = 56 * 1024 * 1024

ROWS_NORM = 256
ROWS_MM = 1024
COLS_MM = 512
ATTN_BLK = 256
ATTN_HEADS = 4
LRU_ROWS = 256
CROSS_ROWS = 256
PAGES_PER_STEP = 8


def _cparams(*sem):
    return pltpu.CompilerParams(dimension_semantics=sem, vmem_limit_bytes=VMEM_LIMIT_BYTES)


def _softplus(x):
    return jnp.maximum(x, 0.0) + jnp.log1p(jnp.exp(-jnp.abs(x)))


def _softplus_sum(x):
    return jnp.maximum(x, 0.0) + jnp.log(1.0 + jnp.exp(-jnp.abs(x)))


def _rms(x, g):
    return x * lax.rsqrt(jnp.mean(x * x, axis=-1, keepdims=True) + EPS) * g


def _silu(x):
    return x * jax.nn.sigmoid(x)


def _split_bf16(x):
    hi = x.astype(BF16)
    lo = (x - hi.astype(F32)).astype(BF16)
    return hi, lo


def _rmsnorm_kernel(x_ref, g_ref, o_ref):
    o_ref[...] = _rms(x_ref[...], g_ref[...]).astype(o_ref.dtype)


def _rmsnorm(x, g, out_dtype):
    m, d = x.shape
    tm = min(m, ROWS_NORM)
    return pl.pallas_call(
        _rmsnorm_kernel,
        out_shape=jax.ShapeDtypeStruct((m, d), out_dtype),
        grid=(m // tm,),
        in_specs=[pl.BlockSpec((tm, d), lambda i: (i, 0)),
                  pl.BlockSpec((1, d), lambda i: (0, 0))],
        out_specs=pl.BlockSpec((tm, d), lambda i: (i, 0)),
        compiler_params=_cparams("parallel"),
        name="rmsnorm",
    )(x, g.reshape(1, d))


def _matmul_kernel(*refs, n_in, has_res):
    o_ref = refs[-1]
    acc = None
    for a_ref, w_ref in zip(refs[:n_in], refs[n_in:2 * n_in]):
        d = jnp.dot(a_ref[...].astype(BF16), w_ref[...].astype(BF16), preferred_element_type=F32)
        acc = d if acc is None else acc + d
    if has_res:
        acc = acc + refs[2 * n_in][...]
    o_ref[...] = acc


def _matmul(a_list, w_list, n, res=None, rows=ROWS_MM, name="matmul"):
    m = a_list[0].shape[0]
    tm = min(m, rows)
    tn = min(n, COLS_MM)
    in_specs, args = [], []
    for a in a_list:
        in_specs.append(pl.BlockSpec((tm, a.shape[1]), lambda i, j: (i, 0)))
        args.append(a)
    for a, (w, layer, rb) in zip(a_list, w_list):
        in_specs.append(pl.BlockSpec((None, a.shape[1], tn),
                                     functools.partial(lambda i, j, layer, rb: (layer, rb, j), layer=layer, rb=rb)))
        args.append(w)
    if res is not None:
        in_specs.append(pl.BlockSpec((tm, tn), lambda i, j: (i, j)))
        args.append(res)
    return pl.pallas_call(
        functools.partial(_matmul_kernel, n_in=len(a_list), has_res=res is not None),
        out_shape=jax.ShapeDtypeStruct((m, n), F32),
        grid=(m // tm, n // tn),
        in_specs=in_specs,
        out_specs=pl.BlockSpec((tm, tn), lambda i, j: (i, j)),
        compiler_params=_cparams("parallel", "arbitrary"),
        name=name,
    )(*args)


def _matmul_pair_kernel(*refs, n_in, has_res, tn):
    a_refs, s_refs, w_refs = refs[:n_in], refs[n_in:2 * n_in], refs[2 * n_in:3 * n_in]
    rest = refs[3 * n_in:]
    res_ref, res_s_ref = (rest[0], rest[1]) if has_res else (None, None)
    o_ref, os_ref = rest[2 * has_res], rest[2 * has_res + 1]

    def product(lhs_refs):
        acc = None
        for a_ref, w_ref in zip(lhs_refs, w_refs):
            d = jnp.dot(a_ref[...], w_ref[...].astype(BF16), preferred_element_type=F32)
            acc = d if acc is None else acc + d
        return acc

    acc = product(a_refs)
    o_ref[...] = acc if res_ref is None else acc + res_ref[...]

    @pl.when(pl.program_id(0) == 0)
    def _():
        cols = pl.ds(pl.multiple_of(pl.program_id(1) * tn, tn), tn)
        acc = product(s_refs)
        os_ref[:, cols] = acc if res_s_ref is None else acc + res_s_ref[:, cols]


def _matmul_pair(a_list, s_list, w_list, n, res=None, res_s=None, name="matmul_pair"):
    m, ms = a_list[0].shape[0], s_list[0].shape[0]
    tm = min(m, ROWS_MM)
    tn = min(n, COLS_MM)
    in_specs, args = [], []
    for a in a_list:
        in_specs.append(pl.BlockSpec((tm, a.shape[1]), lambda i, j: (i, 0)))
        args.append(a)
    for s in s_list:
        in_specs.append(pl.BlockSpec((ms, s.shape[1]), lambda i, j: (0, 0)))
        args.append(s)
    for a, (w, layer, rb) in zip(a_list, w_list):
        in_specs.append(pl.BlockSpec((None, a.shape[1], tn),
                                     functools.partial(lambda i, j, layer, rb: (layer, rb, j), layer=layer, rb=rb)))
        args.append(w)
    if res is not None:
        in_specs += [pl.BlockSpec((tm, tn), lambda i, j: (i, j)), pl.BlockSpec((ms, n), lambda i, j: (0, 0))]
        args += [res, res_s]
    return pl.pallas_call(
        functools.partial(_matmul_pair_kernel, n_in=len(a_list), has_res=res is not None, tn=tn),
        out_shape=(jax.ShapeDtypeStruct((m, n), F32), jax.ShapeDtypeStruct((ms, n), F32)),
        grid=(m // tm, n // tn),
        in_specs=in_specs,
        out_specs=(pl.BlockSpec((tm, tn), lambda i, j: (i, j)), pl.BlockSpec((ms, n), lambda i, j: (0, 0))),
        compiler_params=_cparams("arbitrary", "arbitrary"),
        name=name,
    )(*args)


MASKED_LOGIT = -1e30


def _sb_tail(z, tri):
    hi, lo = _split_bf16(_softplus_sum(z))
    return jnp.dot(hi, tri, preferred_element_type=F32) + jnp.dot(lo, tri, preferred_element_type=F32)


def _sb_prompt_kernel(bias_ref, q_ref, k_ref, v_ref, tri_ref, o_ref, kb_ref, vb_ref, z_ref, tail_ref,
                      *, blk, hd, scale):
    hg = pl.program_id(1)
    i = pl.program_id(2)
    n_h = q_ref.shape[1] // hd

    @pl.when(i == 0)
    def _():
        kb_ref[...] = k_ref[...].astype(BF16)
        vb_ref[...] = v_ref[...].astype(BF16)

    q = [q_ref[:, h * hd:(h + 1) * hd].astype(BF16) for h in range(n_h)]
    bias = [bias_ref[hg * n_h + h] for h in range(n_h)]
    tri = tri_ref[...]

    def rows(j):
        return pl.ds(pl.multiple_of(j * blk, blk), blk)

    def front(j, h, mask):
        kblk = kb_ref[rows(j), h * hd:(h + 1) * hd]
        z = lax.dot_general(q[h], kblk, (((1,), (1,)), ((), ())), preferred_element_type=F32) * scale + bias[h]
        if mask is not None:
            z = jnp.where(mask, z, MASKED_LOGIT)
        z_ref[h] = z
        tail_ref[h] = _sb_tail(z, tri)

    def back(j, h, acc, carry):
        tail = tail_ref[h]
        a = jnp.exp(z_ref[h] - tail - carry)
        pv = jnp.dot(a.astype(BF16), vb_ref[rows(j), h * hd:(h + 1) * hd], preferred_element_type=F32)
        return acc + pv, carry + tail[:, 0:1]

    row = lax.broadcasted_iota(jnp.int32, (blk, blk), 0)
    col = lax.broadcasted_iota(jnp.int32, (blk, blk), 1)
    for h in range(n_h):
        front(i, h, col < row)

    def body(t, state):
        j = i - 1 - t
        state = tuple(back(j + 1, h, *state[h]) for h in range(n_h))
        for h in range(n_h):
            front(j, h, None)
        return state

    zero = (jnp.zeros((blk, hd), F32), jnp.zeros((blk, 1), F32))
    state = lax.fori_loop(0, i, body, (zero,) * n_h)
    for h in range(n_h):
        acc, _ = back(0, h, *state[h])
        o_ref[:, h * hd:(h + 1) * hd] = acc


def _sb_prompt(proj, sb_bias, tri, n_heads, hd):
    b, t, _ = proj.shape
    blk = min(t, ATTN_BLK)
    n_h = ATTN_HEADS
    groups = n_heads // n_h
    w = n_h * hd
    return pl.pallas_call(
        functools.partial(_sb_prompt_kernel, blk=blk, hd=hd, scale=1.0 / math.sqrt(hd)),
        out_shape=jax.ShapeDtypeStruct((b, t, n_heads * hd), F32),
        grid=(b, groups, t // blk),
        in_specs=[pl.BlockSpec(memory_space=pltpu.SMEM),
                  pl.BlockSpec((None, blk, w), lambda bi, hg, i: (bi, i, hg)),
                  pl.BlockSpec((None, t, w), lambda bi, hg, i: (bi, 0, groups + hg)),
                  pl.BlockSpec((None, t, w), lambda bi, hg, i: (bi, 0, 2 * groups + hg)),
                  pl.BlockSpec((blk, blk), lambda bi, hg, i: (0, 0))],
        out_specs=pl.BlockSpec((None, blk, w), lambda bi, hg, i: (bi, i, hg)),
        scratch_shapes=[pltpu.VMEM((t, w), BF16), pltpu.VMEM((t, w), BF16),
                        pltpu.VMEM((n_h, blk, blk), F32), pltpu.VMEM((n_h, blk, blk), F32)],
        compiler_params=_cparams("parallel", "parallel", "arbitrary"),
        name="sb_prompt",
    )(sb_bias, proj, proj, proj, tri)


def _sb_decode_kernel(pt_ref, q_ref, kn_ref, vn_ref, bias_ref, tri_ref, *rest,
                      n_pages_step, n_heads, scale):
    n_tiles = n_heads // SUBLANES
    n_blocks = n_pages_step * n_tiles
    k_refs = rest[:n_blocks]
    v_refs = rest[n_blocks:2 * n_blocks]
    o_ref = rest[2 * n_blocks]
    carry_ref, acc_ref = rest[2 * n_blocks + 1:]
    p = pl.program_id(1)
    page = tri_ref.shape[0]
    rows, hd = q_ref.shape
    slab = rows // n_heads
    tri_t = tri_ref[...]
    bias = bias_ref[...]
    q = q_ref[...].astype(BF16)
    key = lax.broadcasted_iota(jnp.int32, (page, rows), 0)
    qrow = lax.broadcasted_iota(jnp.int32, (page, rows), 1)
    qhead = qrow // slab

    def front(k_head, mask):
        s = None
        for h0 in range(0, n_heads, SUBLANES):
            k_all = jnp.concatenate([k_head(h0 + h).astype(BF16) for h in range(SUBLANES)], axis=0)
            s_all = lax.dot_general(k_all, q, (((1,), (1,)), ((), ())), preferred_element_type=F32)
            for h in range(SUBLANES):
                part = s_all[h * page:(h + 1) * page]
                s = part if s is None else jnp.where(qhead == h0 + h, part, s)
        z = s * scale + bias
        if mask is not None:
            z = jnp.where(mask, z, MASKED_LOGIT)
        hi, lo = _split_bf16(_softplus_sum(z))
        tail = jnp.dot(tri_t, hi, preferred_element_type=F32) + jnp.dot(tri_t, lo, preferred_element_type=F32)
        return z, tail

    def back(z, tail, carry, v_head):
        a = jnp.transpose(jnp.exp(z - tail - carry))
        pv = jnp.concatenate(
            [jnp.dot(a[h * slab:(h + 1) * slab].astype(BF16), v_head(h).astype(BF16), preferred_element_type=F32)
             for h in range(n_heads)], axis=0)
        return pv, carry + tail[0:1, :]

    @pl.when(p == 0)
    def _():
        n_new = kn_ref.shape[0] // n_heads
        pad = jnp.zeros((page - n_new, hd), F32)
        z, tail = front(lambda h: jnp.concatenate([kn_ref[pl.ds(h, n_new, stride=n_heads), :], pad], axis=0),
                        key < (qrow % slab))
        pv, carry = back(z, tail, jnp.zeros((1, rows), F32),
                         lambda h: jnp.concatenate([vn_ref[pl.ds(h, n_new, stride=n_heads), :], pad], axis=0))
        acc_ref[...] = pv
        carry_ref[...] = carry

    def head_of(refs, i, h):
        flat = refs[i * n_tiles + h // SUBLANES].reshape(page * SUBLANES, hd)
        return flat[pl.ds(h % SUBLANES, page, stride=SUBLANES), :]

    fronts = [front(lambda h: head_of(k_refs, i, h), None) for i in range(n_pages_step)]
    carry = carry_ref[...]
    acc = acc_ref[...]
    for i, (z, tail) in enumerate(fronts):
        pv, carry = back(z, tail, carry, lambda h: head_of(v_refs, i, h))
        acc = acc + pv
    carry_ref[...] = carry
    acc_ref[...] = acc

    @pl.when(p == pl.num_programs(1) - 1)
    def _():
        o_ref[...] = acc_ref[...]


def _sb_decode(q_rows, k_new, v_new, cache_k, cache_v, layer, page_table, sb_bias, tri, n_heads):
    b, rows, hd = q_rows.shape
    n_pages = page_table.shape[1]
    page = tri.shape[0]
    pps = PAGES_PER_STEP
    n_tiles = n_heads // SUBLANES
    bias_rows = jnp.repeat(sb_bias, rows // n_heads)[None, :]
    tri = jnp.transpose(tri)

    def page_spec(i):
        return pl.BlockSpec((None, None, page, None, SUBLANES, hd),
                            lambda bi, p, pt: (layer, pt[bi, n_pages - 1 - (p * pps + i // n_tiles)], 0,
                                               i % n_tiles, 0, 0))

    grid_spec = pltpu.PrefetchScalarGridSpec(
        num_scalar_prefetch=1,
        grid=(b, n_pages // pps),
        in_specs=[pl.BlockSpec((None, rows, hd), lambda bi, p, pt: (bi, 0, 0)),
                  pl.BlockSpec((None, k_new.shape[1], hd), lambda bi, p, pt: (bi, 0, 0)),
                  pl.BlockSpec((None, v_new.shape[1], hd), lambda bi, p, pt: (bi, 0, 0)),
                  pl.BlockSpec((1, rows), lambda bi, p, pt: (0, 0)),
                  pl.BlockSpec((page, page), lambda bi, p, pt: (0, 0))]
                 + [page_spec(i) for i in range(pps * n_tiles)] * 2,
        out_specs=pl.BlockSpec((None, rows, hd), lambda bi, p, pt: (bi, 0, 0)),
        scratch_shapes=[pltpu.VMEM((1, rows), F32),
                        pltpu.VMEM((rows, hd), F32)],
    )
    return pl.pallas_call(
        functools.partial(_sb_decode_kernel, n_pages_step=pps, n_heads=n_heads, scale=1.0 / math.sqrt(hd)),
        out_shape=jax.ShapeDtypeStruct((b, rows, hd), F32),
        grid_spec=grid_spec,
        compiler_params=_cparams("parallel", "arbitrary"),
        name="sb_decode",
    )(page_table, q_rows, k_new, v_new, bias_rows, tri,
      *([cache_k] * (pps * n_tiles)), *([cache_v] * (pps * n_tiles)))


def _kv_out_kernel(*refs, n_layers, n_heads, hd):
    ko_ref, vo_ref = refs[2 * n_layers:]
    layer = pl.program_id(0)
    for l in range(n_layers):
        @pl.when(layer == l)
        def _():
            k_ref, v_ref = refs[2 * l], refs[2 * l + 1]
            rows = k_ref.shape[0]
            for h in range(n_heads):
                ko_ref[pl.ds(h, rows, stride=n_heads), :] = k_ref[:, h * hd:(h + 1) * hd]
                vo_ref[pl.ds(h, rows, stride=n_heads), :] = v_ref[:, h * hd:(h + 1) * hd]


def _kv_out(projs, n_heads, hd):
    n_layers = len(projs)
    b, t, _ = projs[0].shape
    wa = n_heads * hd
    tile = min(t, ROWS_NORM)
    n_tiles = t // tile
    out = jax.ShapeDtypeStruct((n_layers, b, t * n_heads, hd), F32)
    out_spec = pl.BlockSpec((None, None, tile * n_heads, hd), lambda l, bi, i: (l, bi, i, 0))

    def in_spec(layer, col):
        def index(l, bi, i):
            before, own = l < layer, l == layer
            return (jnp.where(own, bi, jnp.where(before, 0, b - 1)),
                    jnp.where(own, i, jnp.where(before, 0, n_tiles - 1)), col)
        return pl.BlockSpec((None, tile, wa), index)

    in_specs, args = [], []
    for l, proj in enumerate(projs):
        in_specs += [in_spec(l, 1), in_spec(l, 2)]
        args += [proj, proj]
    return pl.pallas_call(
        functools.partial(_kv_out_kernel, n_layers=n_layers, n_heads=n_heads, hd=hd),
        out_shape=(out, out),
        grid=(n_layers, b, n_tiles),
        in_specs=in_specs,
        out_specs=(out_spec, out_spec),
        compiler_params=_cparams("arbitrary", "arbitrary", "arbitrary"),
        name="kv_out",
    )(*args)


def _gate_kernel(y_ref, z_ref, g_ref, o_ref):
    o_ref[...] = (_rms(y_ref[...], g_ref[...]) * _silu(z_ref[...])).astype(o_ref.dtype)


def _gate(y, proj, z_block, g):
    m, w = y.shape
    tm = min(m, ROWS_NORM)
    return pl.pallas_call(
        _gate_kernel,
        out_shape=jax.ShapeDtypeStruct((m, w), BF16),
        grid=(m // tm,),
        in_specs=[pl.BlockSpec((tm, w), lambda i: (i, 0)),
                  pl.BlockSpec((tm, w), lambda i: (i, z_block)),
                  pl.BlockSpec((1, w), lambda i: (0, 0))],
        out_specs=pl.BlockSpec((tm, w), lambda i: (i, 0)),
        compiler_params=_cparams("parallel"),
        name="gate_a",
    )(y, proj, g)


def _sgu_prompt_kernel(u_ref, v_ref, z_ref, gs_ref, ws_ref, bst_ref, gb_ref, o_ref, *, n_groups, gw):
    chunk = v_ref.shape[0]
    vn = _rms(v_ref[...], gs_ref[...]).astype(BF16)
    row = lax.broadcasted_iota(jnp.int32, (chunk, chunk), 0)
    col = lax.broadcasted_iota(jnp.int32, (chunk, chunk), 1)
    parts = []
    for g in range(n_groups):
        w = jnp.where(col <= row, ws_ref[g], 0.0).astype(BF16)
        mixed = jnp.dot(w, vn[:, g * gw:(g + 1) * gw], preferred_element_type=F32) + bst_ref[:, g:g + 1]
        parts.append(u_ref[:, g * gw:(g + 1) * gw] * mixed)
    yb = jnp.concatenate(parts, axis=1)
    o_ref[...] = (_rms(yb, gb_ref[...]) * _silu(z_ref[...])).astype(o_ref.dtype)


def _sgu_prompt(proj, u_block, g_sgu, w_s, b_s_t, g_b):
    b, t, _ = proj.shape
    n_groups, chunk, _ = w_s.shape
    wb = g_sgu.shape[1]
    assert t % chunk == 0

    def col_spec(blk):
        return pl.BlockSpec((None, chunk, wb), lambda bi, c: (bi, c, blk))

    return pl.pallas_call(
        functools.partial(_sgu_prompt_kernel, n_groups=n_groups, gw=wb // n_groups),
        out_shape=jax.ShapeDtypeStruct((b, t, wb), BF16),
        grid=(b, t // chunk),
        in_specs=[col_spec(u_block), col_spec(u_block + 1), col_spec(u_block + 2),
                  pl.BlockSpec((1, wb), lambda bi, c: (0, 0)),
                  pl.BlockSpec((n_groups, chunk, chunk), lambda bi, c: (0, 0, 0)),
                  pl.BlockSpec((chunk, n_groups), lambda bi, c: (0, 0)),
                  pl.BlockSpec((1, wb), lambda bi, c: (0, 0))],
        out_specs=pl.BlockSpec((None, chunk, wb), lambda bi, c: (bi, c, 0)),
        compiler_params=_cparams("parallel", "parallel"),
        name="sgu_prompt",
    )(proj, proj, proj, g_sgu, w_s, b_s_t, g_b)


def _lru_gates(y, wri_ref, br_ref, bi_ref, lam_ref, n_blocks, bs):
    y16 = y.astype(BF16)
    r_parts, i_parts = [], []
    for n in range(n_blocks):
        ri = jnp.dot(y16[:, n * bs:(n + 1) * bs], wri_ref[n], preferred_element_type=F32)
        r_parts.append(ri[:, :bs])
        i_parts.append(ri[:, bs:])
    r = jax.nn.sigmoid(jnp.concatenate(r_parts, axis=1) + br_ref[...])
    i = jax.nn.sigmoid(jnp.concatenate(i_parts, axis=1) + bi_ref[...])
    log_a = -LRU_C * r * _softplus(-lam_ref[...])
    a = jnp.exp(log_a)
    u = jnp.sqrt(-jnp.tanh(log_a) * (a * a + 1.0)) * (i * y)
    return a, u


def _lru_prompt_kernel(x_ref, z_ref, cw_ref, cb_ref, wri_ref, br_ref, bi_ref, lam_ref, gb_ref,
                       o_ref, h_ref, xp_ref, hc_ref, *, n_blocks, bs, conv_w):
    c = pl.program_id(1)
    tt = x_ref.shape[0]

    @pl.when(c == 0)
    def _():
        xp_ref[0:SUBLANES, :] = jnp.zeros((SUBLANES, xp_ref.shape[1]), F32)
        hc_ref[...] = jnp.zeros_like(hc_ref)

    @pl.when(c > 0)
    def _():
        xp_ref[0:SUBLANES, :] = xp_ref[tt:tt + SUBLANES, :]

    xp_ref[SUBLANES:, :] = x_ref[...]
    y = xp_ref[SUBLANES:, :] * cw_ref[conv_w - 1:conv_w, :] + cb_ref[...]
    for j in range(conv_w - 1):
        shift = conv_w - 1 - j
        y = y + xp_ref[pl.ds(SUBLANES - shift, tt), :] * cw_ref[j:j + 1, :]

    a, u = _lru_gates(y, wri_ref, br_ref, bi_ref, lam_ref, n_blocks, bs)

    row = lax.broadcasted_iota(jnp.int32, a.shape, 0)
    k = 1
    while k < tt:
        a_prev = jnp.where(row >= k, pltpu.roll(a, k, 0), 1.0)
        u_prev = jnp.where(row >= k, pltpu.roll(u, k, 0), 0.0)
        u = a * u_prev + u
        a = a * a_prev
        k *= 2
    h = a * hc_ref[SUBLANES - 1:SUBLANES, :] + u
    hc_ref[...] = h[tt - SUBLANES:, :]
    h_ref[...] = h[tt - SUBLANES:, :]
    o_ref[...] = (_rms(h, gb_ref[...]) * _silu(z_ref[...])).astype(o_ref.dtype)


def _lru_prompt(proj, x_block, conv_w, conv_b, wri, b_r, b_i, lam, g_c):
    b, t, _ = proj.shape
    wc = conv_b.shape[1]
    n_blocks, bs, _ = wri.shape
    tt = min(t, LRU_ROWS)
    cw = conv_w.shape[0]
    vec = pl.BlockSpec((1, wc), lambda bi, c: (0, 0))
    return pl.pallas_call(
        functools.partial(_lru_prompt_kernel, n_blocks=n_blocks, bs=bs, conv_w=cw),
        out_shape=(jax.ShapeDtypeStruct((b, t, wc), BF16),
                   jax.ShapeDtypeStruct((b, SUBLANES, wc), F32)),
        grid=(b, t // tt),
        in_specs=[pl.BlockSpec((None, tt, wc), lambda bi, c: (bi, c, x_block)),
                  pl.BlockSpec((None, tt, wc), lambda bi, c: (bi, c, x_block + 1)),
                  pl.BlockSpec((cw, wc), lambda bi, c: (0, 0)),
                  vec,
                  pl.BlockSpec((n_blocks, bs, 2 * bs), lambda bi, c: (0, 0, 0)),
                  vec, vec, vec, vec],
        out_specs=(pl.BlockSpec((None, tt, wc), lambda bi, c: (bi, c, 0)),
                   pl.BlockSpec((None, SUBLANES, wc), lambda bi, c: (bi, 0, 0))),
        scratch_shapes=[pltpu.VMEM((tt + SUBLANES, wc), F32), pltpu.VMEM((SUBLANES, wc), F32)],
        compiler_params=_cparams("parallel", "arbitrary"),
        name="lru_prompt",
    )(proj, proj, conv_w, conv_b, wri, b_r, b_i, lam, g_c)


def _sample_bc_kernel(u_ref, v_ref, zb_ref, x_ref, zc_ref, gs_ref, wbig_ref, bst_ref, gb_ref,
                      sc_ref, h0_ref, cw_ref, cb_ref, wri_ref, br_ref, bi_ref, lam_ref, gc_ref,
                      ob_ref, oc_ref, vn_ref, h_ref, conv_ref, *, n_groups, gw, n_blocks, bs, conv_w, n_b):
    rows = v_ref.shape[0]
    n_t = rows // n_b
    kpad = wbig_ref.shape[2]
    vn = _rms(v_ref[...], gs_ref[...])
    vn_ref[...] = vn
    vpad = jnp.concatenate([vn, jnp.zeros((kpad - rows, vn.shape[1]), F32)], axis=0).astype(BF16)
    r = lax.broadcasted_iota(jnp.int32, (rows, kpad), 0)
    c = lax.broadcasted_iota(jnp.int32, (rows, kpad), 1)
    keep = (c < rows) & ((c % n_b) == (r % n_b)) & ((c // n_b) <= (r // n_b))
    parts = []
    for g in range(n_groups):
        w = jnp.where(keep, wbig_ref[g], 0.0).astype(BF16)
        mixed = jnp.dot(w, vpad[:, g * gw:(g + 1) * gw], preferred_element_type=F32) + bst_ref[:, g:g + 1]
        parts.append(u_ref[:, g * gw:(g + 1) * gw] * mixed)
    yb = jnp.concatenate(parts, axis=1)
    ob_ref[...] = (_rms(yb, gb_ref[...]) * _silu(zb_ref[...])).astype(ob_ref.dtype)

    full = jnp.concatenate([sc_ref[...], x_ref[...]], axis=0)
    conv_ref[...] = full[rows:, :]
    ys = []
    for t in range(n_t):
        def slab(m):
            return full[m * n_b:(m + 1) * n_b, :]
        y = slab(t + conv_w - 1) * cw_ref[conv_w - 1:conv_w, :] + cb_ref[...]
        for j in range(conv_w - 1):
            y = y + slab(t + j) * cw_ref[j:j + 1, :]
        ys.append(y)
    y = jnp.concatenate(ys, axis=0)
    a, u = _lru_gates(y, wri_ref, br_ref, bi_ref, lam_ref, n_blocks, bs)
    h = h0_ref[...]
    hs = []
    for t in range(n_t):
        h = a[t * n_b:(t + 1) * n_b, :] * h + u[t * n_b:(t + 1) * n_b, :]
        hs.append(h)
    h_ref[...] = h
    yc = jnp.concatenate(hs, axis=0)
    oc_ref[...] = (_rms(yc, gc_ref[...]) * _silu(zc_ref[...])).astype(oc_ref.dtype)


def _sample_bc(proj, u_block, x_block, g_sgu, wbig, bst, g_b, sc, h0, conv_w, conv_b, wri, b_r, b_i, lam, g_c):
    rows = proj.shape[0]
    wb = g_sgu.shape[1]
    wc = conv_b.shape[1]
    n_b = h0.shape[0]
    n_groups = wbig.shape[0]
    n_blocks, bs, _ = wri.shape
    cw = conv_w.shape[0]

    def full(shape):
        return pl.BlockSpec(shape, lambda i: (0,) * len(shape))

    return pl.pallas_call(
        functools.partial(_sample_bc_kernel, n_groups=n_groups, gw=wb // n_groups, n_blocks=n_blocks,
                          bs=bs, conv_w=cw, n_b=n_b),
        out_shape=(jax.ShapeDtypeStruct((rows, wb), BF16),
                   jax.ShapeDtypeStruct((rows, wc), BF16),
                   jax.ShapeDtypeStruct((rows, wb), F32),
                   jax.ShapeDtypeStruct((n_b, wc), F32),
                   jax.ShapeDtypeStruct(sc.shape, F32)),
        grid=(1,),
        in_specs=[pl.BlockSpec((rows, wb), lambda i: (0, u_block)),
                  pl.BlockSpec((rows, wb), lambda i: (0, u_block + 1)),
                  pl.BlockSpec((rows, wb), lambda i: (0, u_block + 2)),
                  pl.BlockSpec((rows, wc), lambda i: (0, x_block)),
                  pl.BlockSpec((rows, wc), lambda i: (0, x_block + 1)),
                  full((1, wb)), full(wbig.shape), full(bst.shape), full((1, wb)),
                  full(sc.shape), full(h0.shape), full(conv_w.shape), full((1, wc)),
                  full(wri.shape), full((1, wc)), full((1, wc)), full((1, wc)), full((1, wc))],
        out_specs=(full((rows, wb)), full((rows, wc)), full((rows, wb)), full((n_b, wc)), full(sc.shape)),
        compiler_params=_cparams("arbitrary"),
        name="sample_bc",
    )(proj, proj, proj, proj, proj, g_sgu, wbig, bst, g_b, sc, h0, conv_w, conv_b, wri, b_r, b_i, lam, g_c)


def _cross_kernel(x_ref, g_ref, wq_ref, mk_ref, mv_ref, wo_ref, gf_ref, o_ref, *, n_heads, hd, n_b, final):
    x = x_ref[...]
    hn = _rms(x, g_ref[...]).astype(BF16)
    q = jnp.dot(hn, wq_ref[...].astype(BF16), preferred_element_type=F32)
    n_mem_rows = mk_ref.shape[0]
    rows = x.shape[0]
    if n_b:
        r = lax.broadcasted_iota(jnp.int32, (rows, n_mem_rows), 0)
        c = lax.broadcasted_iota(jnp.int32, (rows, n_mem_rows), 1)
        own = (c // (n_mem_rows // n_b)) == (r % n_b)
    outs = []
    for h in range(n_heads):
        qh = q[:, h * hd:(h + 1) * hd].astype(BF16)
        kh = mk_ref[:, h * hd:(h + 1) * hd].astype(BF16)
        vh = mv_ref[:, h * hd:(h + 1) * hd].astype(BF16)
        s = lax.dot_general(qh, kh, (((1,), (1,)), ((), ())), preferred_element_type=F32)
        s = s * (1.0 / math.sqrt(hd))
        if n_b:
            s = jnp.where(own, s, -jnp.inf)
        e = jnp.exp(s - jnp.max(s, axis=-1, keepdims=True))
        p = e / jnp.sum(e, axis=-1, keepdims=True)
        outs.append(jnp.dot(p.astype(BF16), vh, preferred_element_type=F32))
    o = jnp.concatenate(outs, axis=1).astype(BF16)
    y = x + jnp.dot(o, wo_ref[...].astype(BF16), preferred_element_type=F32)
    if final:
        y = _rms(y, gf_ref[...])
    o_ref[...] = y


def _cross(x, g, w_cq, mk, mv, w_co, layer, g_final, final, rows_per_batch, n_heads, hd):
    m, d = x.shape
    wx = n_heads * hd
    if rows_per_batch:
        tm = min(rows_per_batch, CROSS_ROWS)
        tiles_per_batch = rows_per_batch // tm
        n_mem = mk.shape[0] // (m // rows_per_batch)
        mem_spec = pl.BlockSpec((n_mem, wx), lambda i: (i // tiles_per_batch, 0))
        n_b = 0
    else:
        tm = m
        mem_spec = pl.BlockSpec(mk.shape, lambda i: (0, 0))
        n_b = SUBLANES
    return pl.pallas_call(
        functools.partial(_cross_kernel, n_heads=n_heads, hd=hd, n_b=n_b, final=final),
        out_shape=jax.ShapeDtypeStruct((m, d), F32),
        grid=(m // tm,),
        in_specs=[pl.BlockSpec((tm, d), lambda i: (i, 0)),
                  pl.BlockSpec((1, d), lambda i: (0, 0)),
                  pl.BlockSpec((None, d, wx), lambda i: (layer, 0, 0)),
                  mem_spec, mem_spec,
                  pl.BlockSpec((None, wx, d), lambda i: (layer, 0, 0)),
                  pl.BlockSpec((1, d), lambda i: (0, 0))],
        out_specs=pl.BlockSpec((tm, d), lambda i: (i, 0)),
        compiler_params=_cparams("parallel"),
        name="cross",
    )(x, g, w_cq, mk, mv, w_co, g_final)


def kernel(x_prompt, x_sample, mem_prompt, cache_k, cache_v, page_table, cache_mem_k, cache_mem_v, state_h, state_conv, g_mix, w_in, g_branch, w_out, g_sgu, w_s, b_s, conv_w, conv_b, w_r, b_r, w_i, b_i, lru_lambda, sb_bias, g_cross, w_cq, w_ck, w_cv, w_co, g_final):
    bp, seq, d = x_prompt.shape
    db, dt, _ = x_sample.shape
    depth, n_pool, page, n_heads, hd = cache_k.shape
    n_mem, hx, hdx = cache_mem_k.shape[2:]
    wa = n_heads * hd
    wb = g_sgu.shape[1]
    wc = state_h.shape[2]
    wx = hx * hdx
    n_groups, chunk = w_s.shape[1], w_s.shape[2]
    n_blocks, bs = w_r.shape[1], w_r.shape[2]
    cw = conv_w.shape[1]
    n_in = w_in.shape[2]
    assert n_in == 4 * wa + 3 * wb + 2 * wc and (4 * wa) % wb == 0 and (4 * wa + 3 * wb) % wc == 0
    assert db == SUBLANES and dt <= SUBLANES and dt <= chunk and cw - 1 <= SUBLANES
    u_block = 4 * wa // wb
    x_block = (4 * wa + 3 * wb) // wc
    z_block = 3

    assert n_heads % SUBLANES == 0
    cache_k2 = cache_k.reshape(depth, n_pool, page, n_heads // SUBLANES, SUBLANES, hd)
    cache_v2 = cache_v.reshape(depth, n_pool, page, n_heads // SUBLANES, SUBLANES, hd)
    mem_x = mem_prompt.reshape(bp * n_mem, d)
    w_ckv = jnp.concatenate([w_ck, w_cv], axis=2)
    w_cq = w_cq.astype(BF16)
    w_co = w_co.astype(BF16)
    wri = jnp.concatenate([w_r, w_i], axis=3).astype(BF16)
    g_fin = g_final.reshape(1, d)
    idx = jnp.arange(page)
    tri_page = (idx[:, None] >= idx[None, :]).astype(BF16)
    blk = min(seq, ATTN_BLK)
    idx = jnp.arange(blk)
    tri_blk = (idx[:, None] >= idx[None, :]).astype(BF16)

    xp = x_prompt.reshape(bp * seq, d)
    xs = jnp.swapaxes(x_sample, 0, 1).reshape(dt * db, d)

    outs = {k: [] for k in ("ks", "vs", "mk", "mv", "hp", "hs", "cp", "cs", "sgu")}
    projs = []
    for l in range(depth):
        gm = g_mix[l].reshape(1, d)
        g_a = g_branch[l, :wa].reshape(1, wa)
        g_b = g_branch[l, wa:wa + wb].reshape(1, wb)
        g_c = g_branch[l, wa + wb:].reshape(1, wc)
        gs = g_sgu[l].reshape(1, wb)
        cb = conv_b[l].reshape(1, wc)
        br = b_r[l].reshape(1, wc)
        bi = b_i[l].reshape(1, wc)
        lam = lru_lambda[l].reshape(1, wc)
        gc = g_cross[l].reshape(1, d)
        final = l == depth - 1
        w_out_l = [(w_out, l, 0), (w_out, l, wa // wb), (w_out, l, (wa + wb) // wc)]

        hn = _rmsnorm(xp, gm, BF16)
        hn_s = _rmsnorm(xs, gm, BF16)
        proj, proj_s = _matmul_pair([hn], [hn_s], [(w_in, l, 0)], n_in, name="in_proj")
        proj3 = proj.reshape(bp, seq, n_in)

        ya = _sb_prompt(proj3, sb_bias[l], tri_blk, n_heads, hd)
        cat_a = _gate(ya.reshape(bp * seq, wa), proj, z_block, g_a)
        cat_b = _sgu_prompt(proj3, u_block, gs, w_s[l], jnp.transpose(b_s[l]), g_b)
        cat_c, h_p = _lru_prompt(proj3, x_block, conv_w[l], cb, wri[l], br, bi, lam, g_c)
        projs.append(proj3)
        outs["hp"].append(h_p[:, SUBLANES - 1, :])
        outs["cp"].append(proj3[:, seq - (cw - 1):, x_block * wc:(x_block + 1) * wc])

        def batch_major(a):
            return jnp.swapaxes(a.reshape(dt, db, -1), 0, 1)

        q_s = batch_major(proj_s[:, :wa]).reshape(db, dt, n_heads, hd)
        k_s = batch_major(proj_s[:, wa:2 * wa])
        v_s = batch_major(proj_s[:, 2 * wa:3 * wa])
        pad_t = ((0, 0), (0, 0), (0, SUBLANES - dt), (0, 0))
        q_rows = jnp.pad(jnp.swapaxes(q_s, 1, 2), pad_t).reshape(db, n_heads * SUBLANES, hd)
        pad_j = ((0, 0), (0, (SUBLANES - dt) * n_heads), (0, 0))
        ya_s = _sb_decode(q_rows, jnp.pad(k_s.reshape(db, dt * n_heads, hd), pad_j),
                          jnp.pad(v_s.reshape(db, dt * n_heads, hd), pad_j), cache_k2, cache_v2, l, page_table,
                          sb_bias[l], tri_page, n_heads)
        ya_s = ya_s.reshape(db, n_heads, SUBLANES, hd)[:, :, :dt]
        ya_s = jnp.transpose(ya_s, (2, 0, 1, 3)).reshape(dt * db, wa)
        cat_a_s = _gate(ya_s, proj_s, z_block, g_a)
        wbig = jnp.repeat(jnp.repeat(w_s[l][:, :dt, :dt], db, axis=1), db, axis=2)
        wbig = jnp.pad(wbig, ((0, 0), (0, 0), (0, LANES - dt * db)))
        bst = jnp.repeat(jnp.transpose(b_s[l][:, :dt]), db, axis=0)
        sc = jnp.swapaxes(state_conv[l], 0, 1).reshape((cw - 1) * db, wc)
        cat_b_s, cat_c_s, vn_s, h_s, conv_s = _sample_bc(
            proj_s, u_block, x_block, gs, wbig, bst, g_b, sc, state_h[l], conv_w[l], cb, wri[l], br, bi, lam, g_c)

        x1, x1_s = _matmul_pair([cat_a, cat_b.reshape(bp * seq, wb), cat_c.reshape(bp * seq, wc)],
                                [cat_a_s, cat_b_s, cat_c_s], w_out_l, d, res=xp, res_s=xs, name="out_proj")
        mkv = _matmul([mem_x], [(w_ckv, l, 0)], 2 * wx, rows=ROWS_NORM, name="mem_proj")
        mk, mv = mkv[:, :wx], mkv[:, wx:]
        xp = _cross(x1, gc, w_cq, mk, mv, w_co, l, g_fin, final, seq, hx, hdx)
        xs = _cross(x1_s, gc, w_cq, cache_mem_k[l].reshape(db * n_mem, wx), cache_mem_v[l].reshape(db * n_mem, wx),
                    w_co, l, g_fin, final, 0, hx, hdx)
        outs["mk"].append(mk.reshape(bp, n_mem, hx, hdx))
        outs["mv"].append(mv.reshape(bp, n_mem, hx, hdx))
        outs["ks"].append(k_s.reshape(db, dt, n_heads, hd))
        outs["vs"].append(v_s.reshape(db, dt, n_heads, hd))
        outs["hs"].append(h_s)
        outs["cs"].append(jnp.swapaxes(conv_s.reshape(cw - 1, db, wc), 0, 1))
        outs["sgu"].append(batch_major(vn_s))

    y_prompt = xp.reshape(bp, seq, d)
    y_sample = jnp.swapaxes(xs.reshape(dt, db, d), 0, 1)
    st = {k: jnp.stack(v) for k, v in outs.items()}
    k_p, v_p = _kv_out(projs, n_heads, hd)
    st["kp"] = k_p.reshape(depth, bp, seq, n_heads, hd)
    st["vp"] = v_p.reshape(depth, bp, seq, n_heads, hd)
    return (y_prompt, y_sample, st["kp"], st["vp"], st["ks"], st["vs"], st["mk"], st["mv"],
            st["hp"], st["hs"], st["cp"], st["cs"], st["sgu"])
```

```python
import functools
import math

import jax
import jax.numpy as jnp
from jax import lax
from jax.experimental import pallas as pl
from jax.experimental.pallas import tpu as pltpu

F32 = jnp.float32
BF16 = jnp.bfloat16

EPS = 1e-6
LRU_C = 8.0
SUBLANES = 8
LANES = 128
VMEM_LIMIT_BYTES = 56 * 1024 * 1024

ROWS_NORM = 256
ROWS_MM = 1024
COLS_MM = 512
ATTN_BLK = 256
ATTN_HEADS = 4
LRU_ROWS = 256
CROSS_ROWS = 256
PAGES_PER_STEP = 8


def _cparams(*sem):
    return pltpu.CompilerParams(dimension_semantics=sem, vmem_limit_bytes=VMEM_LIMIT_BYTES)


def _softplus(x):
    return jnp.maximum(x, 0.0) + jnp.log1p(jnp.exp(-jnp.abs(x)))


def _softplus_sum(x):
    return jnp.maximum(x, 0.0) + jnp.log(1.0 + jnp.exp2(jnp.abs(x) * (-math.log2(math.e))))


def _rms(x, g):
    return x * lax.rsqrt(jnp.mean(x * x, axis=-1, keepdims=True) + EPS) * g


def _silu(x):
    return x * jax.nn.sigmoid(x)


def _split_bf16(x):
    hi = x.astype(BF16)
    lo = (x - hi.astype(F32)).astype(BF16)
    return hi, lo


def _rmsnorm_kernel(x_ref, g_ref, o_ref):
    o_ref[...] = _rms(x_ref[...], g_ref[...]).astype(o_ref.dtype)


def _rmsnorm(x, g, out_dtype):
    m, d = x.shape
    tm = min(m, ROWS_NORM)
    return pl.pallas_call(
        _rmsnorm_kernel,
        out_shape=jax.ShapeDtypeStruct((m, d), out_dtype),
        grid=(m // tm,),
        in_specs=[pl.BlockSpec((tm, d), lambda i: (i, 0)),
                  pl.BlockSpec((1, d), lambda i: (0, 0))],
        out_specs=pl.BlockSpec((tm, d), lambda i: (i, 0)),
        compiler_params=_cparams("parallel"),
        name="rmsnorm",
    )(x, g.reshape(1, d))


def _matmul_kernel(*refs, n_in, has_res):
    o_ref = refs[-1]
    acc = None
    for a_ref, w_ref in zip(refs[:n_in], refs[n_in:2 * n_in]):
        d = jnp.dot(a_ref[...].astype(BF16), w_ref[...].astype(BF16), preferred_element_type=F32)
        acc = d if acc is None else acc + d
    if has_res:
        acc = acc + refs[2 * n_in][...]
    o_ref[...] = acc


def _matmul(a_list, w_list, n, res=None, rows=ROWS_MM, name="matmul"):
    m = a_list[0].shape[0]
    tm = min(m, rows)
    tn = min(n, COLS_MM)
    in_specs, args = [], []
    for a in a_list:
        in_specs.append(pl.BlockSpec((tm, a.shape[1]), lambda i, j: (i, 0)))
        args.append(a)
    for a, (w, layer, rb) in zip(a_list, w_list):
        in_specs.append(pl.BlockSpec((None, a.shape[1], tn),
                                     functools.partial(lambda i, j, layer, rb: (layer, rb, j), layer=layer, rb=rb)))
        args.append(w)
    if res is not None:
        in_specs.append(pl.BlockSpec((tm, tn), lambda i, j: (i, j)))
        args.append(res)
    return pl.pallas_call(
        functools.partial(_matmul_kernel, n_in=len(a_list), has_res=res is not None),
        out_shape=jax.ShapeDtypeStruct((m, n), F32),
        grid=(m // tm, n // tn),
        in_specs=in_specs,
        out_specs=pl.BlockSpec((tm, tn), lambda i, j: (i, j)),
        compiler_params=_cparams("parallel", "arbitrary"),
        name=name,
    )(*args)


def _matmul_pair_kernel(*refs, n_in, has_res, tn):
    a_refs, s_refs, w_refs = refs[:n_in], refs[n_in:2 * n_in], refs[2 * n_in:3 * n_in]
    rest = refs[3 * n_in:]
    res_ref, res_s_ref = (rest[0], rest[1]) if has_res else (None, None)
    o_ref, os_ref = rest[2 * has_res], rest[2 * has_res + 1]

    def product(lhs_refs):
        acc = None
        for a_ref, w_ref in zip(lhs_refs, w_refs):
            d = jnp.dot(a_ref[...], w_ref[...].astype(BF16), preferred_element_type=F32)
            acc = d if acc is None else acc + d
        return acc

    acc = product(a_refs)
    o_ref[...] = acc if res_ref is None else acc + res_ref[...]

    @pl.when(pl.program_id(0) == 0)
    def _():
        cols = pl.ds(pl.multiple_of(pl.program_id(1) * tn, tn), tn)
        acc = product(s_refs)
        os_ref[:, cols] = acc if res_s_ref is None else acc + res_s_ref[:, cols]


def _matmul_pair(a_list, s_list, w_list, n, res=None, res_s=None, name="matmul_pair"):
    m, ms = a_list[0].shape[0], s_list[0].shape[0]
    tm = min(m, ROWS_MM)
    tn = min(n, COLS_MM)
    in_specs, args = [], []
    for a in a_list:
        in_specs.append(pl.BlockSpec((tm, a.shape[1]), lambda i, j: (i, 0)))
        args.append(a)
    for s in s_list:
        in_specs.append(pl.BlockSpec((ms, s.shape[1]), lambda i, j: (0, 0)))
        args.append(s)
    for a, (w, layer, rb) in zip(a_list, w_list):
        in_specs.append(pl.BlockSpec((None, a.shape[1], tn),
                                     functools.partial(lambda i, j, layer, rb: (layer, rb, j), layer=layer, rb=rb)))
        args.append(w)
    if res is not None:
        in_specs += [pl.BlockSpec((tm, tn), lambda i, j: (i, j)), pl.BlockSpec((ms, n), lambda i, j: (0, 0))]
        args += [res, res_s]
    return pl.pallas_call(
        functools.partial(_matmul_pair_kernel, n_in=len(a_list), has_res=res is not None, tn=tn),
        out_shape=(jax.ShapeDtypeStruct((m, n), F32), jax.ShapeDtypeStruct((ms, n), F32)),
        grid=(m // tm, n // tn),
        in_specs=in_specs,
        out_specs=(pl.BlockSpec((tm, tn), lambda i, j: (i, j)), pl.BlockSpec((ms, n), lambda i, j: (0, 0))),
        compiler_params=_cparams("arbitrary", "arbitrary"),
        name=name,
    )(*args)


MASKED_LOGIT = -1e30


def _sb_tail(z, tri):
    hi, lo = _split_bf16(_softplus_sum(z))
    both = jnp.dot(jnp.concatenate([hi, lo], axis=0), tri, preferred_element_type=F32)
    return both[:z.shape[0]] + both[z.shape[0]:]


def _sb_prompt_kernel(bias_ref, q_ref, k_ref, v_ref, tri_ref, o_ref, kb_ref, vb_ref, d_ref, mass_ref,
                      *, blk, hd, scale):
    hg = pl.program_id(1)
    i = pl.program_id(2)
    n_h = q_ref.shape[1] // hd

    @pl.when(i == 0)
    def _():
        kb_ref[...] = k_ref[...].astype(BF16)
        vb_ref[...] = v_ref[...].astype(BF16)

    q = [q_ref[:, h * hd:(h + 1) * hd].astype(BF16) for h in range(n_h)]
    bias = [bias_ref[hg * n_h + h] for h in range(n_h)]
    tri = tri_ref[...]

    def rows(j):
        return pl.ds(pl.multiple_of(j * blk, blk), blk)

    def front(j, h, mask):
        kblk = kb_ref[rows(j), h * hd:(h + 1) * hd]
        z = lax.dot_general(q[h], kblk, (((1,), (1,)), ((), ())), preferred_element_type=F32) * scale + bias[h]
        if mask is not None:
            z = jnp.where(mask, z, MASKED_LOGIT)
        tail = _sb_tail(z, tri)
        d_ref[h] = z - tail
        mass_ref[h] = tail[:, 0:1]

    def back(j, h, acc, carry):
        a = jnp.exp(d_ref[h] - carry)
        pv = jnp.dot(a.astype(BF16), vb_ref[rows(j), h * hd:(h + 1) * hd], preferred_element_type=F32)
        return acc + pv, carry + mass_ref[h]

    row = lax.broadcasted_iota(jnp.int32, (blk, blk), 0)
    col = lax.broadcasted_iota(jnp.int32, (blk, blk), 1)
    for h in range(n_h):
        front(i, h, col < row)

    def body(t, state):
        j = i - 1 - t
        state = tuple(back(j + 1, h, *state[h]) for h in range(n_h))
        for h in range(n_h):
            front(j, h, None)
        return state

    zero = (jnp.zeros((blk, hd), F32), jnp.zeros((blk, 1), F32))
    state = lax.fori_loop(0, i, body, (zero,) * n_h)
    for h in range(n_h):
        acc, _ = back(0, h, *state[h])
        o_ref[:, h * hd:(h + 1) * hd] = acc


def _sb_prompt(proj, sb_bias, tri, n_heads, hd):
    b, t, _ = proj.shape
    blk = min(t, ATTN_BLK)
    n_h = ATTN_HEADS
    groups = n_heads // n_h
    w = n_h * hd
    return pl.pallas_call(
        functools.partial(_sb_prompt_kernel, blk=blk, hd=hd, scale=1.0 / math.sqrt(hd)),
        out_shape=jax.ShapeDtypeStruct((b, t, n_heads * hd), F32),
        grid=(b, groups, t // blk),
        in_specs=[pl.BlockSpec(memory_space=pltpu.SMEM),
                  pl.BlockSpec((None, blk, w), lambda bi, hg, i: (bi, i, hg)),
                  pl.BlockSpec((None, t, w), lambda bi, hg, i: (bi, 0, groups + hg)),
                  pl.BlockSpec((None, t, w), lambda bi, hg, i: (bi, 0, 2 * groups + hg)),
                  pl.BlockSpec((blk, blk), lambda bi, hg, i: (0, 0))],
        out_specs=pl.BlockSpec((None, blk, w), lambda bi, hg, i: (bi, i, hg)),
        scratch_shapes=[pltpu.VMEM((t, w), BF16), pltpu.VMEM((t, w), BF16),
                        pltpu.VMEM((n_h, blk, blk), F32), pltpu.VMEM((n_h, blk, 1), F32)],
        compiler_params=_cparams("parallel", "parallel", "arbitrary"),
        name="sb_prompt",
    )(sb_bias, proj, proj, proj, tri)


def _sb_decode_kernel(pt_ref, q_ref, kn_ref, vn_ref, bias_ref, tri_ref, *rest,
                      n_pages_step, n_heads, scale):
    n_tiles = n_heads // SUBLANES
    n_blocks = n_pages_step * n_tiles
    k_refs = rest[:n_blocks]
    v_refs = rest[n_blocks:2 * n_blocks]
    o_ref = rest[2 * n_blocks]
    carry_ref, acc_ref = rest[2 * n_blocks + 1:]
    p = pl.program_id(1)
    page = tri_ref.shape[0]
    rows, hd = q_ref.shape
    slab = rows // n_heads
    tri_t = tri_ref[...]
    bias = bias_ref[...]
    q = q_ref[...].astype(BF16)
    key = lax.broadcasted_iota(jnp.int32, (page, rows), 0)
    qrow = lax.broadcasted_iota(jnp.int32, (page, rows), 1)
    qhead = qrow // slab

    def front(k_head, mask):
        s = None
        for h0 in range(0, n_heads, SUBLANES):
            k_all = jnp.concatenate([k_head(h0 + h).astype(BF16) for h in range(SUBLANES)], axis=0)
            s_all = lax.dot_general(k_all, q, (((1,), (1,)), ((), ())), preferred_element_type=F32)
            for h in range(SUBLANES):
                part = s_all[h * page:(h + 1) * page]
                s = part if s is None else jnp.where(qhead == h0 + h, part, s)
        z = s * scale + bias
        if mask is not None:
            z = jnp.where(mask, z, MASKED_LOGIT)
        hi, lo = _split_bf16(_softplus_sum(z))
        tail = jnp.dot(tri_t, hi, preferred_element_type=F32) + jnp.dot(tri_t, lo, preferred_element_type=F32)
        return z, tail

    def back(z, tail, carry, v_head):
        a = jnp.transpose(jnp.exp(z - tail - carry))
        pv = jnp.concatenate(
            [jnp.dot(a[h * slab:(h + 1) * slab].astype(BF16), v_head(h).astype(BF16), preferred_element_type=F32)
             for h in range(n_heads)], axis=0)
        return pv, carry + tail[0:1, :]

    @pl.when(p == 0)
    def _():
        n_new = kn_ref.shape[0] // n_heads
        pad = jnp.zeros((page - n_new, hd), F32)
        z, tail = front(lambda h: jnp.concatenate([kn_ref[pl.ds(h, n_new, stride=n_heads), :], pad], axis=0),
                        key < (qrow % slab))
        pv, carry = back(z, tail, jnp.zeros((1, rows), F32),
                         lambda h: jnp.concatenate([vn_ref[pl.ds(h, n_new, stride=n_heads), :], pad], axis=0))
        acc_ref[...] = pv
        carry_ref[...] = carry

    def head_of(refs, i, h):
        flat = refs[i * n_tiles + h // SUBLANES].reshape(page * SUBLANES, hd)
        return flat[pl.ds(h % SUBLANES, page, stride=SUBLANES), :]

    fronts = [front(lambda h: head_of(k_refs, i, h), None) for i in range(n_pages_step)]
    carry = carry_ref[...]
    acc = acc_ref[...]
    for i, (z, tail) in enumerate(fronts):
        pv, carry = back(z, tail, carry, lambda h: head_of(v_refs, i, h))
        acc = acc + pv
    carry_ref[...] = carry
    acc_ref[...] = acc

    @pl.when(p == pl.num_programs(1) - 1)
    def _():
        o_ref[...] = acc_ref[...]


def _sb_decode(q_rows, k_new, v_new, cache_k, cache_v, layer, page_table, sb_bias, tri, n_heads):
    b, rows, hd = q_rows.shape
    n_pages = page_table.shape[1]
    page = tri.shape[0]
    pps = PAGES_PER_STEP
    n_tiles = n_heads // SUBLANES
    bias_rows = jnp.repeat(sb_bias, rows // n_heads)[None, :]
    tri = jnp.transpose(tri)

    def page_spec(i):
        return pl.BlockSpec((None, None, page, None, SUBLANES, hd),
                            lambda bi, p, pt: (layer, pt[bi, n_pages - 1 - (p * pps + i // n_tiles)], 0,
                                               i % n_tiles, 0, 0))

    grid_spec = pltpu.PrefetchScalarGridSpec(
        num_scalar_prefetch=1,
        grid=(b, n_pages // pps),
        in_specs=[pl.BlockSpec((None, rows, hd), lambda bi, p, pt: (bi, 0, 0)),
                  pl.BlockSpec((None, k_new.shape[1], hd), lambda bi, p, pt: (bi, 0, 0)),
                  pl.BlockSpec((None, v_new.shape[1], hd), lambda bi, p, pt: (bi, 0, 0)),
                  pl.BlockSpec((1, rows), lambda bi, p, pt: (0, 0)),
                  pl.BlockSpec((page, page), lambda bi, p, pt: (0, 0))]
                 + [page_spec(i) for i in range(pps * n_tiles)] * 2,
        out_specs=pl.BlockSpec((None, rows, hd), lambda bi, p, pt: (bi, 0, 0)),
        scratch_shapes=[pltpu.VMEM((1, rows), F32),
                        pltpu.VMEM((rows, hd), F32)],
    )
    return pl.pallas_call(
        functools.partial(_sb_decode_kernel, n_pages_step=pps, n_heads=n_heads, scale=1.0 / math.sqrt(hd)),
        out_shape=jax.ShapeDtypeStruct((b, rows, hd), F32),
        grid_spec=grid_spec,
        compiler_params=_cparams("parallel", "arbitrary"),
        name="sb_decode",
    )(page_table, q_rows, k_new, v_new, bias_rows, tri,
      *([cache_k] * (pps * n_tiles)), *([cache_v] * (pps * n_tiles)))


def _kv_out_kernel(*refs, n_layers, n_heads, hd):
    ko_ref, vo_ref = refs[2 * n_layers:]
    layer = pl.program_id(0)
    for l in range(n_layers):
        @pl.when(layer == l)
        def _():
            k_ref, v_ref = refs[2 * l], refs[2 * l + 1]
            rows = k_ref.shape[0]
            for h in range(n_heads):
                ko_ref[pl.ds(h, rows, stride=n_heads), :] = k_ref[:, h * hd:(h + 1) * hd]
                vo_ref[pl.ds(h, rows, stride=n_heads), :] = v_ref[:, h * hd:(h + 1) * hd]


def _kv_out(projs, n_heads, hd):
    n_layers = len(projs)
    b, t, _ = projs[0].shape
    wa = n_heads * hd
    tile = min(t, ROWS_NORM)
    n_tiles = t // tile
    out = jax.ShapeDtypeStruct((n_layers, b, t * n_heads, hd), F32)
    out_spec = pl.BlockSpec((None, None, tile * n_heads, hd), lambda l, bi, i: (l, bi, i, 0))

    def in_spec(layer, col):
        def index(l, bi, i):
            before, own = l < layer, l == layer
            return (jnp.where(own, bi, jnp.where(before, 0, b - 1)),
                    jnp.where(own, i, jnp.where(before, 0, n_tiles - 1)), col)
        return pl.BlockSpec((None, tile, wa), index)

    in_specs, args = [], []
    for l, proj in enumerate(projs):
        in_specs += [in_spec(l, 1), in_spec(l, 2)]
        args += [proj, proj]
    return pl.pallas_call(
        functools.partial(_kv_out_kernel, n_layers=n_layers, n_heads=n_heads, hd=hd),
        out_shape=(out, out),
        grid=(n_layers, b, n_tiles),
        in_specs=in_specs,
        out_specs=(out_spec, out_spec),
        compiler_params=_cparams("arbitrary", "arbitrary", "arbitrary"),
        name="kv_out",
    )(*args)


def _gate_kernel(y_ref, z_ref, g_ref, o_ref):
    o_ref[...] = (_rms(y_ref[...], g_ref[...]) * _silu(z_ref[...])).astype(o_ref.dtype)


def _gate(y, proj, z_block, g):
    m, w = y.shape
    tm = min(m, ROWS_NORM)
    return pl.pallas_call(
        _gate_kernel,
        out_shape=jax.ShapeDtypeStruct((m, w), BF16),
        grid=(m // tm,),
        in_specs=[pl.BlockSpec((tm, w), lambda i: (i, 0)),
                  pl.BlockSpec((tm, w), lambda i: (i, z_block)),
                  pl.BlockSpec((1, w), lambda i: (0, 0))],
        out_specs=pl.BlockSpec((tm, w), lambda i: (i, 0)),
        compiler_params=_cparams("parallel"),
        name="gate_a",
    )(y, proj, g)


def _sgu_prompt_kernel(u_ref, v_ref, z_ref, gs_ref, ws_ref, bst_ref, gb_ref, o_ref, *, n_groups, gw):
    chunk = v_ref.shape[0]
    vn = _rms(v_ref[...], gs_ref[...]).astype(BF16)
    row = lax.broadcasted_iota(jnp.int32, (chunk, chunk), 0)
    col = lax.broadcasted_iota(jnp.int32, (chunk, chunk), 1)
    parts = []
    for g in range(n_groups):
        w = jnp.where(col <= row, ws_ref[g], 0.0).astype(BF16)
        mixed = jnp.dot(w, vn[:, g * gw:(g + 1) * gw], preferred_element_type=F32) + bst_ref[:, g:g + 1]
        parts.append(u_ref[:, g * gw:(g + 1) * gw] * mixed)
    yb = jnp.concatenate(parts, axis=1)
    o_ref[...] = (_rms(yb, gb_ref[...]) * _silu(z_ref[...])).astype(o_ref.dtype)


def _sgu_prompt(proj, u_block, g_sgu, w_s, b_s_t, g_b):
    b, t, _ = proj.shape
    n_groups, chunk, _ = w_s.shape
    wb = g_sgu.shape[1]
    assert t % chunk == 0

    def col_spec(blk):
        return pl.BlockSpec((None, chunk, wb), lambda bi, c: (bi, c, blk))

    return pl.pallas_call(
        functools.partial(_sgu_prompt_kernel, n_groups=n_groups, gw=wb // n_groups),
        out_shape=jax.ShapeDtypeStruct((b, t, wb), BF16),
        grid=(b, t // chunk),
        in_specs=[col_spec(u_block), col_spec(u_block + 1), col_spec(u_block + 2),
                  pl.BlockSpec((1, wb), lambda bi, c: (0, 0)),
                  pl.BlockSpec((n_groups, chunk, chunk), lambda bi, c: (0, 0, 0)),
                  pl.BlockSpec((chunk, n_groups), lambda bi, c: (0, 0)),
                  pl.BlockSpec((1, wb), lambda bi, c: (0, 0))],
        out_specs=pl.BlockSpec((None, chunk, wb), lambda bi, c: (bi, c, 0)),
        compiler_params=_cparams("parallel", "parallel"),
        name="sgu_prompt",
    )(proj, proj, proj, g_sgu, w_s, b_s_t, g_b)


def _lru_gates(y, wri_ref, br_ref, bi_ref, lam_ref, n_blocks, bs):
    y16 = y.astype(BF16)
    r_parts, i_parts = [], []
    for n in range(n_blocks):
        ri = jnp.dot(y16[:, n * bs:(n + 1) * bs], wri_ref[n], preferred_element_type=F32)
        r_parts.append(ri[:, :bs])
        i_parts.append(ri[:, bs:])
    r = jax.nn.sigmoid(jnp.concatenate(r_parts, axis=1) + br_ref[...])
    i = jax.nn.sigmoid(jnp.concatenate(i_parts, axis=1) + bi_ref[...])
    log_a = -LRU_C * r * _softplus(-lam_ref[...])
    a = jnp.exp(log_a)
    u = jnp.sqrt(-jnp.tanh(log_a) * (a * a + 1.0)) * (i * y)
    return a, u


def _lru_prompt_kernel(x_ref, z_ref, cw_ref, cb_ref, wri_ref, br_ref, bi_ref, lam_ref, gb_ref,
                       o_ref, h_ref, xp_ref, hc_ref, *, n_blocks, bs, conv_w):
    c = pl.program_id(1)
    tt = x_ref.shape[0]

    @pl.when(c == 0)
    def _():
        xp_ref[0:SUBLANES, :] = jnp.zeros((SUBLANES, xp_ref.shape[1]), F32)
        hc_ref[...] = jnp.zeros_like(hc_ref)

    @pl.when(c > 0)
    def _():
        xp_ref[0:SUBLANES, :] = xp_ref[tt:tt + SUBLANES, :]

    xp_ref[SUBLANES:, :] = x_ref[...]
    y = xp_ref[SUBLANES:, :] * cw_ref[conv_w - 1:conv_w, :] + cb_ref[...]
    for j in range(conv_w - 1):
        shift = conv_w - 1 - j
        y = y + xp_ref[pl.ds(SUBLANES - shift, tt), :] * cw_ref[j:j + 1, :]

    a, u = _lru_gates(y, wri_ref, br_ref, bi_ref, lam_ref, n_blocks, bs)

    row = lax.broadcasted_iota(jnp.int32, a.shape, 0)
    k = 1
    while k < tt:
        a_prev = jnp.where(row >= k, pltpu.roll(a, k, 0), 1.0)
        u_prev = jnp.where(row >= k, pltpu.roll(u, k, 0), 0.0)
        u = a * u_prev + u
        a = a * a_prev
        k *= 2
    h = a * hc_ref[SUBLANES - 1:SUBLANES, :] + u
    hc_ref[...] = h[tt - SUBLANES:, :]
    h_ref[...] = h[tt - SUBLANES:, :]
    o_ref[...] = (_rms(h, gb_ref[...]) * _silu(z_ref[...])).astype(o_ref.dtype)


def _lru_prompt(proj, x_block, conv_w, conv_b, wri, b_r, b_i, lam, g_c):
    b, t, _ = proj.shape
    wc = conv_b.shape[1]
    n_blocks, bs, _ = wri.shape
    tt = min(t, LRU_ROWS)
    cw = conv_w.shape[0]
    vec = pl.BlockSpec((1, wc), lambda bi, c: (0, 0))
    return pl.pallas_call(
        functools.partial(_lru_prompt_kernel, n_blocks=n_blocks, bs=bs, conv_w=cw),
        out_shape=(jax.ShapeDtypeStruct((b, t, wc), BF16),
                   jax.ShapeDtypeStruct((b, SUBLANES, wc), F32)),
        grid=(b, t // tt),
        in_specs=[pl.BlockSpec((None, tt, wc), lambda bi, c: (bi, c, x_block)),
                  pl.BlockSpec((None, tt, wc), lambda bi, c: (bi, c, x_block + 1)),
                  pl.BlockSpec((cw, wc), lambda bi, c: (0, 0)),
                  vec,
                  pl.BlockSpec((n_blocks, bs, 2 * bs), lambda bi, c: (0, 0, 0)),
                  vec, vec, vec, vec],
        out_specs=(pl.BlockSpec((None, tt, wc), lambda bi, c: (bi, c, 0)),
                   pl.BlockSpec((None, SUBLANES, wc), lambda bi, c: (bi, 0, 0))),
        scratch_shapes=[pltpu.VMEM((tt + SUBLANES, wc), F32), pltpu.VMEM((SUBLANES, wc), F32)],
        compiler_params=_cparams("parallel", "arbitrary"),
        name="lru_prompt",
    )(proj, proj, conv_w, conv_b, wri, b_r, b_i, lam, g_c)


def _sample_bc_kernel(u_ref, v_ref, zb_ref, x_ref, zc_ref, gs_ref, wbig_ref, bst_ref, gb_ref,
                      sc_ref, h0_ref, cw_ref, cb_ref, wri_ref, br_ref, bi_ref, lam_ref, gc_ref,
                      ob_ref, oc_ref, vn_ref, h_ref, conv_ref, *, n_groups, gw, n_blocks, bs, conv_w, n_b):
    rows = v_ref.shape[0]
    n_t = rows // n_b
    kpad = wbig_ref.shape[2]
    vn = _rms(v_ref[...], gs_ref[...])
    vn_ref[...] = vn
    vpad = jnp.concatenate([vn, jnp.zeros((kpad - rows, vn.shape[1]), F32)], axis=0).astype(BF16)
    r = lax.broadcasted_iota(jnp.int32, (rows, kpad), 0)
    c = lax.broadcasted_iota(jnp.int32, (rows, kpad), 1)
    keep = (c < rows) & ((c % n_b) == (r % n_b)) & ((c // n_b) <= (r // n_b))
    parts = []
    for g in range(n_groups):
        w = jnp.where(keep, wbig_ref[g], 0.0).astype(BF16)
        mixed = jnp.dot(w, vpad[:, g * gw:(g + 1) * gw], preferred_element_type=F32) + bst_ref[:, g:g + 1]
        parts.append(u_ref[:, g * gw:(g + 1) * gw] * mixed)
    yb = jnp.concatenate(parts, axis=1)
    ob_ref[...] = (_rms(yb, gb_ref[...]) * _silu(zb_ref[...])).astype(ob_ref.dtype)

    full = jnp.concatenate([sc_ref[...], x_ref[...]], axis=0)
    conv_ref[...] = full[rows:, :]
    ys = []
    for t in range(n_t):
        def slab(m):
            return full[m * n_b:(m + 1) * n_b, :]
        y = slab(t + conv_w - 1) * cw_ref[conv_w - 1:conv_w, :] + cb_ref[...]
        for j in range(conv_w - 1):
            y = y + slab(t + j) * cw_ref[j:j + 1, :]
        ys.append(y)
    y = jnp.concatenate(ys, axis=0)
    a, u = _lru_gates(y, wri_ref, br_ref, bi_ref, lam_ref, n_blocks, bs)
    h = h0_ref[...]
    hs = []
    for t in range(n_t):
        h = a[t * n_b:(t + 1) * n_b, :] * h + u[t * n_b:(t + 1) * n_b, :]
        hs.append(h)
    h_ref[...] = h
    yc = jnp.concatenate(hs, axis=0)
    oc_ref[...] = (_rms(yc, gc_ref[...]) * _silu(zc_ref[...])).astype(oc_ref.dtype)


def _sample_bc(proj, u_block, x_block, g_sgu, wbig, bst, g_b, sc, h0, conv_w, conv_b, wri, b_r, b_i, lam, g_c):
    rows = proj.shape[0]
    wb = g_sgu.shape[1]
    wc = conv_b.shape[1]
    n_b = h0.shape[0]
    n_groups = wbig.shape[0]
    n_blocks, bs, _ = wri.shape
    cw = conv_w.shape[0]

    def full(shape):
        return pl.BlockSpec(shape, lambda i: (0,) * len(shape))

    return pl.pallas_call(
        functools.partial(_sample_bc_kernel, n_groups=n_groups, gw=wb // n_groups, n_blocks=n_blocks,
                          bs=bs, conv_w=cw, n_b=n_b),
        out_shape=(jax.ShapeDtypeStruct((rows, wb), BF16),
                   jax.ShapeDtypeStruct((rows, wc), BF16),
                   jax.ShapeDtypeStruct((rows, wb), F32),
                   jax.ShapeDtypeStruct((n_b, wc), F32),
                   jax.ShapeDtypeStruct(sc.shape, F32)),
        grid=(1,),
        in_specs=[pl.BlockSpec((rows, wb), lambda i: (0, u_block)),
                  pl.BlockSpec((rows, wb), lambda i: (0, u_block + 1)),
                  pl.BlockSpec((rows, wb), lambda i: (0, u_block + 2)),
                  pl.BlockSpec((rows, wc), lambda i: (0, x_block)),
                  pl.BlockSpec((rows, wc), lambda i: (0, x_block + 1)),
                  full((1, wb)), full(wbig.shape), full(bst.shape), full((1, wb)),
                  full(sc.shape), full(h0.shape), full(conv_w.shape), full((1, wc)),
                  full(wri.shape), full((1, wc)), full((1, wc)), full((1, wc)), full((1, wc))],
        out_specs=(full((rows, wb)), full((rows, wc)), full((rows, wb)), full((n_b, wc)), full(sc.shape)),
        compiler_params=_cparams("arbitrary"),
        name="sample_bc",
    )(proj, proj, proj, proj, proj, g_sgu, wbig, bst, g_b, sc, h0, conv_w, conv_b, wri, b_r, b_i, lam, g_c)


def _cross_kernel(x_ref, g_ref, wq_ref, mk_ref, mv_ref, wo_ref, gf_ref, o_ref, *, n_heads, hd, n_b, final):
    x = x_ref[...]
    hn = _rms(x, g_ref[...]).astype(BF16)
    q = jnp.dot(hn, wq_ref[...].astype(BF16), preferred_element_type=F32)
    n_mem_rows = mk_ref.shape[0]
    rows = x.shape[0]
    if n_b:
        r = lax.broadcasted_iota(jnp.int32, (rows, n_mem_rows), 0)
        c = lax.broadcasted_iota(jnp.int32, (rows, n_mem_rows), 1)
        own = (c // (n_mem_rows // n_b)) == (r % n_b)
    outs = []
    for h in range(n_heads):
        qh = q[:, h * hd:(h + 1) * hd].astype(BF16)
        kh = mk_ref[:, h * hd:(h + 1) * hd].astype(BF16)
        vh = mv_ref[:, h * hd:(h + 1) * hd].astype(BF16)
        s = lax.dot_general(qh, kh, (((1,), (1,)), ((), ())), preferred_element_type=F32)
        s = s * (1.0 / math.sqrt(hd))
        if n_b:
            s = jnp.where(own, s, -jnp.inf)
        e = jnp.exp(s - jnp.max(s, axis=-1, keepdims=True))
        p = e / jnp.sum(e, axis=-1, keepdims=True)
        outs.append(jnp.dot(p.astype(BF16), vh, preferred_element_type=F32))
    o = jnp.concatenate(outs, axis=1).astype(BF16)
    y = x + jnp.dot(o, wo_ref[...].astype(BF16), preferred_element_type=F32)
    if final:
        y = _rms(y, gf_ref[...])
    o_ref[...] = y


def _cross(x, g, w_cq, mk, mv, w_co, layer, g_final, final, rows_per_batch, n_heads, hd):
    m, d = x.shape
    wx = n_heads * hd
    if rows_per_batch:
        tm = min(rows_per_batch, CROSS_ROWS)
        tiles_per_batch = rows_per_batch // tm
        n_mem = mk.shape[0] // (m // rows_per_batch)
        mem_spec = pl.BlockSpec((n_mem, wx), lambda i: (i // tiles_per_batch, 0))
        n_b = 0
    else:
        tm = m
        mem_spec = pl.BlockSpec(mk.shape, lambda i: (0, 0))
        n_b = SUBLANES
    return pl.pallas_call(
        functools.partial(_cross_kernel, n_heads=n_heads, hd=hd, n_b=n_b, final=final),
        out_shape=jax.ShapeDtypeStruct((m, d), F32),
        grid=(m // tm,),
        in_specs=[pl.BlockSpec((tm, d), lambda i: (i, 0)),
                  pl.BlockSpec((1, d), lambda i: (0, 0)),
                  pl.BlockSpec((None, d, wx), lambda i: (layer, 0, 0)),
                  mem_spec, mem_spec,
                  pl.BlockSpec((None, wx, d), lambda i: (layer, 0, 0)),
                  pl.BlockSpec((1, d), lambda i: (0, 0))],
        out_specs=pl.BlockSpec((tm, d), lambda i: (i, 0)),
        compiler_params=_cparams("parallel"),
        name="cross",
    )(x, g, w_cq, mk, mv, w_co, g_final)


def kernel(x_prompt, x_sample, mem_prompt, cache_k, cache_v, page_table, cache_mem_k, cache_mem_v, state_h, state_conv, g_mix, w_in, g_branch, w_out, g_sgu, w_s, b_s, conv_w, conv_b, w_r, b_r, w_i, b_i, lru_lambda, sb_bias, g_cross, w_cq, w_ck, w_cv, w_co, g_final):
    bp, seq, d = x_prompt.shape
    db, dt, _ = x_sample.shape
    depth, n_pool, page, n_heads, hd = cache_k.shape
    n_mem, hx, hdx = cache_mem_k.shape[2:]
    wa = n_heads * hd
    wb = g_sgu.shape[1]
    wc = state_h.shape[2]
    wx = hx * hdx
    n_groups, chunk = w_s.shape[1], w_s.shape[2]
    n_blocks, bs = w_r.shape[1], w_r.shape[2]
    cw = conv_w.shape[1]
    n_in = w_in.shape[2]
    assert n_in == 4 * wa + 3 * wb + 2 * wc and (4 * wa) % wb == 0 and (4 * wa + 3 * wb) % wc == 0
    assert db == SUBLANES and dt <= SUBLANES and dt <= chunk and cw - 1 <= SUBLANES
    u_block = 4 * wa // wb
    x_block = (4 * wa + 3 * wb) // wc
    z_block = 3

    assert n_heads % SUBLANES == 0
    cache_k2 = cache_k.reshape(depth, n_pool, page, n_heads // SUBLANES, SUBLANES, hd)
    cache_v2 = cache_v.reshape(depth, n_pool, page, n_heads // SUBLANES, SUBLANES, hd)
    mem_x = mem_prompt.reshape(bp * n_mem, d)
    w_ckv = jnp.concatenate([w_ck, w_cv], axis=2)
    w_cq = w_cq.astype(BF16)
    w_co = w_co.astype(BF16)
    wri = jnp.concatenate([w_r, w_i], axis=3).astype(BF16)
    g_fin = g_final.reshape(1, d)
    idx = jnp.arange(page)
    tri_page = (idx[:, None] >= idx[None, :]).astype(BF16)
    blk = min(seq, ATTN_BLK)
    idx = jnp.arange(blk)
    tri_blk = (idx[:, None] >= idx[None, :]).astype(BF16)

    xp = x_prompt.reshape(bp * seq, d)
    xs = jnp.swapaxes(x_sample, 0, 1).reshape(dt * db, d)

    outs = {k: [] for k in ("ks", "vs", "mk", "mv", "hp", "hs", "cp", "cs", "sgu")}
    projs = []
    for l in range(depth):
        gm = g_mix[l].reshape(1, d)
        g_a = g_branch[l, :wa].reshape(1, wa)
        g_b = g_branch[l, wa:wa + wb].reshape(1, wb)
        g_c = g_branch[l, wa + wb:].reshape(1, wc)
        gs = g_sgu[l].reshape(1, wb)
        cb = conv_b[l].reshape(1, wc)
        br = b_r[l].reshape(1, wc)
        bi = b_i[l].reshape(1, wc)
        lam = lru_lambda[l].reshape(1, wc)
        gc = g_cross[l].reshape(1, d)
        final = l == depth - 1
        w_out_l = [(w_out, l, 0), (w_out, l, wa // wb), (w_out, l, (wa + wb) // wc)]

        hn = _rmsnorm(xp, gm, BF16)
        hn_s = _rmsnorm(xs, gm, BF16)
        proj, proj_s = _matmul_pair([hn], [hn_s], [(w_in, l, 0)], n_in, name="in_proj")
        proj3 = proj.reshape(bp, seq, n_in)

        ya = _sb_prompt(proj3, sb_bias[l], tri_blk, n_heads, hd)
        cat_a = _gate(ya.reshape(bp * seq, wa), proj, z_block, g_a)
        cat_b = _sgu_prompt(proj3, u_block, gs, w_s[l], jnp.transpose(b_s[l]), g_b)
        cat_c, h_p = _lru_prompt(proj3, x_block, conv_w[l], cb, wri[l], br, bi, lam, g_c)
        projs.append(proj3)
        outs["hp"].append(h_p[:, SUBLANES - 1, :])
        outs["cp"].append(proj3[:, seq - (cw - 1):, x_block * wc:(x_block + 1) * wc])

        def batch_major(a):
            return jnp.swapaxes(a.reshape(dt, db, -1), 0, 1)

        q_s = batch_major(proj_s[:, :wa]).reshape(db, dt, n_heads, hd)
        k_s = batch_major(proj_s[:, wa:2 * wa])
        v_s = batch_major(proj_s[:, 2 * wa:3 * wa])
        pad_t = ((0, 0), (0, 0), (0, SUBLANES - dt), (0, 0))
        q_rows = jnp.pad(jnp.swapaxes(q_s, 1, 2), pad_t).reshape(db, n_heads * SUBLANES, hd)
        pad_j = ((0, 0), (0, (SUBLANES - dt) * n_heads), (0, 0))
        ya_s = _sb_decode(q_rows, jnp.pad(k_s.reshape(db, dt * n_heads, hd), pad_j),
                          jnp.pad(v_s.reshape(db, dt * n_heads, hd), pad_j), cache_k2, cache_v2, l, page_table,
                          sb_bias[l], tri_page, n_heads)
        ya_s = ya_s.reshape(db, n_heads, SUBLANES, hd)[:, :, :dt]
        ya_s = jnp.transpose(ya_s, (2, 0, 1, 3)).reshape(dt * db, wa)
        cat_a_s = _gate(ya_s, proj_s, z_block, g_a)
        wbig = jnp.repeat(jnp.repeat(w_s[l][:, :dt, :dt], db, axis=1), db, axis=2)
        wbig = jnp.pad(wbig, ((0, 0), (0, 0), (0, LANES - dt * db)))
        bst = jnp.repeat(jnp.transpose(b_s[l][:, :dt]), db, axis=0)
        sc = jnp.swapaxes(state_conv[l], 0, 1).reshape((cw - 1) * db, wc)
        cat_b_s, cat_c_s, vn_s, h_s, conv_s = _sample_bc(
            proj_s, u_block, x_block, gs, wbig, bst, g_b, sc, state_h[l], conv_w[l], cb, wri[l], br, bi, lam, g_c)

        x1, x1_s = _matmul_pair([cat_a, cat_b.reshape(bp * seq, wb), cat_c.reshape(bp * seq, wc)],
                                [cat_a_s, cat_b_s, cat_c_s], w_out_l, d, res=xp, res_s=xs, name="out_proj")
        mkv = _matmul([mem_x], [(w_ckv, l, 0)], 2 * wx, rows=ROWS_NORM, name="mem_proj")
        mk, mv = mkv[:, :wx], mkv[:, wx:]
        xp = _cross(x1, gc, w_cq, mk, mv, w_co, l, g_fin, final, seq, hx, hdx)
        xs = _cross(x1_s, gc, w_cq, cache_mem_k[l].reshape(db * n_mem, wx), cache_mem_v[l].reshape(db * n_mem, wx),
                    w_co, l, g_fin, final, 0, hx, hdx)
        outs["mk"].append(mk.reshape(bp, n_mem, hx, hdx))
        outs["mv"].append(mv.reshape(bp, n_mem, hx, hdx))
        outs["ks"].append(k_s.reshape(db, dt, n_heads, hd))
        outs["vs"].append(v_s.reshape(db, dt, n_heads, hd))
        outs["hs"].append(h_s)
        outs["cs"].append(jnp.swapaxes(conv_s.reshape(cw - 1, db, wc), 0, 1))
        outs["sgu"].append(batch_major(vn_s))

    y_prompt = xp.reshape(bp, seq, d)
    y_sample = jnp.swapaxes(xs.reshape(dt, db, d), 0, 1)
    st = {k: jnp.stack(v) for k, v in outs.items()}
    k_p, v_p = _kv_out(projs, n_heads, hd)
    st["kp"] = k_p.reshape(depth, bp, seq, n_heads, hd)
    st["vp"] = v_p.reshape(depth, bp, seq, n_heads, hd)
    return (y_prompt, y_sample, st["kp"], st["vp"], st["ks"], st["vs"], st["mk"], st["mv"],
            st["hp"], st["hs"], st["cp"], st["cs"], st["sgu"])
```
